```python
import jax, jax.numpy as jnp
from jax import lax
import numpy as np

D_MODEL = 2048
BATCH = 2
SEQ = 8192
DEPTH = 2

RMS_EPS = 1e-6
ROPE_THETA = 10000.0
D_FF = 5632
Q_BLOCK = 128

MLA_HEADS = 8
MLA_Q_LORA = 512
MLA_KV_LORA = 256
MLA_NOPE_DIM = 128
MLA_ROPE_DIM = 64
MLA_V_DIM = 128

SWA_HEADS = 8
SWA_KV_HEADS = 2
SWA_HEAD_DIM = 64
WINDOW = 128

FOX_HEADS = 8
FOX_HEAD_DIM = 64

IN_SPLITS = [
    MLA_Q_LORA,
    MLA_KV_LORA,
    MLA_ROPE_DIM,
    SWA_HEADS * SWA_HEAD_DIM,
    SWA_KV_HEADS * SWA_HEAD_DIM,
    SWA_KV_HEADS * SWA_HEAD_DIM,
    FOX_HEADS * FOX_HEAD_DIM,
    FOX_HEADS * FOX_HEAD_DIM,
    FOX_HEADS * FOX_HEAD_DIM,
    FOX_HEADS,
]
IN_COLS = int(sum(IN_SPLITS))
IN_OFFSETS = [int(o) for o in np.cumsum(IN_SPLITS)[:-1]]
MIX_WIDTH = MLA_HEADS * MLA_V_DIM + SWA_HEADS * SWA_HEAD_DIM + FOX_HEADS * FOX_HEAD_DIM

kernel_name = "hybrid_mla_swa_sink_fox_macaron"


def rmsnorm(x, g):
    x32 = x.astype(jnp.float32)
    y = x32 * lax.rsqrt(jnp.mean(x32 * x32, axis=-1, keepdims=True) + RMS_EPS)
    return (y * g.astype(jnp.float32)).astype(x.dtype)


def swiglu(h, w_gate, w_up, w_down):
    return (jax.nn.silu(h @ w_gate) * (h @ w_up)) @ w_down


def rope_tables(positions, dim):
    inv_freq = ROPE_THETA ** (-jnp.arange(0, dim, 2, dtype=jnp.float32) / dim)
    ang = positions.astype(jnp.float32)[..., None] * inv_freq
    return jnp.cos(ang), jnp.sin(ang)


def apply_rope(x, cos, sin):
    half = x.shape[-1] // 2
    x1, x2 = x[..., :half], x[..., half:]
    c, s = cos[:, :, None, :], sin[:, :, None, :]
    return jnp.concatenate([x1 * c - x2 * s, x2 * c + x1 * s], axis=-1).astype(x.dtype)


def causal_block_attention(q, k, v, scale, log_f_cum=None):
    B, S, H, dq = q.shape
    dv = v.shape[-1]
    n = S // Q_BLOCK
    qb = q.reshape(B, n, Q_BLOCK, H, dq).transpose(1, 0, 2, 3, 4)
    key_pos = jnp.arange(S)
    idx = jnp.arange(n)

    def scores_for(i, qi):
        s = jnp.einsum('bqhd,bkhd->bhqk', qi, k).astype(jnp.float32) * scale
        q_pos = i * Q_BLOCK + jnp.arange(Q_BLOCK)
        return s, key_pos[None, :] <= q_pos[:, None]

    def finish(s, mask):
        s = jnp.where(mask[None, None], s, -jnp.inf)
        p = jax.nn.softmax(s, axis=-1).astype(v.dtype)
        return jnp.einsum('bhqk,bkhd->bqhd', p, v)

    if log_f_cum is None:
        def step(args):
            i, qi = args
            s, mask = scores_for(i, qi)
            return finish(s, mask)
        out = lax.map(step, (idx, qb))
    else:
        c_all = log_f_cum.transpose(0, 2, 1)
        cb = log_f_cum.reshape(B, n, Q_BLOCK, H).transpose(1, 0, 3, 2)

        def step(args):
            i, qi, ci = args
            s, mask = scores_for(i, qi)
            s = s + (ci[..., :, None] - c_all[:, :, None, :])
            return finish(s, mask)
        out = lax.map(step, (idx, qb, cb))
    return out.transpose(1, 0, 2, 3, 4).reshape(B, S, H, dv)


def sliding_window_sink_attention(q, k, v, sinks):
    B, S, H, d = q.shape
    Hkv = k.shape[2]
    G = H // Hkv
    n = S // WINDOW
    qb = q.reshape(B, n, WINDOW, Hkv, G, d)
    pad = jnp.zeros((B, WINDOW, Hkv, d), k.dtype)
    kp = jnp.concatenate([pad, k], axis=1).reshape(B, n + 1, WINDOW, Hkv, d)
    vp = jnp.concatenate([pad.astype(v.dtype), v], axis=1).reshape(B, n + 1, WINDOW, Hkv, d)
    kw = jnp.concatenate([kp[:, :-1], kp[:, 1:]], axis=2)
    vw = jnp.concatenate([vp[:, :-1], vp[:, 1:]], axis=2)
    s = jnp.einsum('bnqhgd,bnkhd->bnhgqk', qb, kw).astype(jnp.float32) * (d ** -0.5)
    blk = jnp.arange(n)[:, None, None]
    q_pos = blk * WINDOW + jnp.arange(WINDOW)[None, :, None]
    k_pos = (blk - 1) * WINDOW + jnp.arange(2 * WINDOW)[None, None, :]
    mask = (k_pos <= q_pos) & (k_pos > q_pos - WINDOW) & (k_pos >= 0)
    s = jnp.where(mask[None, :, None, None], s, -jnp.inf)
    sink = jnp.broadcast_to(sinks.astype(jnp.float32).reshape(Hkv, G)[None, None, :, :, None, None],
                            s.shape[:-1] + (1,))
    p = jax.nn.softmax(jnp.concatenate([s, sink], axis=-1), axis=-1)[..., :-1]
    o = jnp.einsum('bnhgqk,bnkhd->bnqhgd', p.astype(v.dtype), vw)
    return o.reshape(B, S, H * d)


def hybrid_mixer(h, cos_m, sin_m, cos_s, sin_s, w_in, q_norm, w_q_b, kv_norm, w_kv_b,
                 sinks, forget_bias, w_out):
    B, S, _ = h.shape
    proj = h @ w_in
    (c_q, c_kv, k_rope, q_s, k_s, v_s, q_f, k_f, v_f, f_logit) = jnp.split(proj, IN_OFFSETS, axis=-1)

    q = (rmsnorm(c_q, q_norm) @ w_q_b).reshape(B, S, MLA_HEADS, MLA_NOPE_DIM + MLA_ROPE_DIM)
    q_nope, q_pe = q[..., :MLA_NOPE_DIM], apply_rope(q[..., MLA_NOPE_DIM:], cos_m, sin_m)
    kv = (rmsnorm(c_kv, kv_norm) @ w_kv_b).reshape(B, S, MLA_HEADS, MLA_NOPE_DIM + MLA_V_DIM)
    k_nope, v_m = kv[..., :MLA_NOPE_DIM], kv[..., MLA_NOPE_DIM:]
    k_pe = apply_rope(k_rope[:, :, None, :], cos_m, sin_m)
    q_m = jnp.concatenate([q_nope, q_pe], axis=-1)
    k_m = jnp.concatenate([k_nope, jnp.broadcast_to(k_pe, (B, S, MLA_HEADS, MLA_ROPE_DIM))], axis=-1)
    o_mla = causal_block_attention(q_m, k_m, v_m, (MLA_NOPE_DIM + MLA_ROPE_DIM) ** -0.5)

    q_s = apply_rope(q_s.reshape(B, S, SWA_HEADS, SWA_HEAD_DIM), cos_s, sin_s)
    k_s = apply_rope(k_s.reshape(B, S, SWA_KV_HEADS, SWA_HEAD_DIM), cos_s, sin_s)
    v_s = v_s.reshape(B, S, SWA_KV_HEADS, SWA_HEAD_DIM)
    o_swa = sliding_window_sink_attention(q_s, k_s, v_s, sinks)

    log_f = jax.nn.log_sigmoid(f_logit.astype(jnp.float32) + forget_bias.astype(jnp.float32))
    c = jnp.cumsum(log_f, axis=1)
    o_fox = causal_block_attention(q_f.reshape(B, S, FOX_HEADS, FOX_HEAD_DIM),
                                   k_f.reshape(B, S, FOX_HEADS, FOX_HEAD_DIM),
                                   v_f.reshape(B, S, FOX_HEADS, FOX_HEAD_DIM),
                                   FOX_HEAD_DIM ** -0.5, c)

    mixed = jnp.concatenate([o_mla.reshape(B, S, -1), o_swa, o_fox.reshape(B, S, -1)], axis=-1)
    return mixed @ w_out


def setup_inputs(seed: int = 0) -> dict:
    key = jax.random.key(seed)
    ks = jax.random.split(key, 24)
    f32 = jnp.float32

    def w(k, fan_in, fan_out):
        return jax.random.normal(k, (DEPTH, fan_in, fan_out), f32) * fan_in ** -0.5

    def gain(k, dim):
        return 1.0 + 0.1 * jax.random.normal(k, (DEPTH, dim), f32)

    return {
        "x": jax.random.normal(ks[0], (BATCH, SEQ, D_MODEL), f32),
        "positions": jnp.broadcast_to(jnp.arange(SEQ, dtype=jnp.int32), (BATCH, SEQ)),
        "ffn1_norm": gain(ks[1], D_MODEL),
        "ffn1_w_gate": w(ks[2], D_MODEL, D_FF),
        "ffn1_w_up": w(ks[3], D_MODEL, D_FF),
        "ffn1_w_down": w(ks[4], D_FF, D_MODEL),
        "mix_norm": gain(ks[5], D_MODEL),
        "w_in": w(ks[6], D_MODEL, IN_COLS),
        "mla_q_norm": gain(ks[7], MLA_Q_LORA),
        "mla_w_q_b": w(ks[8], MLA_Q_LORA, MLA_HEADS * (MLA_NOPE_DIM + MLA_ROPE_DIM)),
        "mla_kv_norm": gain(ks[9], MLA_KV_LORA),
        "mla_w_kv_b": w(ks[10], MLA_KV_LORA, MLA_HEADS * (MLA_NOPE_DIM + MLA_V_DIM)),
        "swa_sinks": 0.5 * jax.random.normal(ks[11], (DEPTH, SWA_HEADS), f32),
        "fox_forget_bias": 0.1 * jax.random.normal(ks[12], (DEPTH, FOX_HEADS), f32),
        "w_out": w(ks[13], MIX_WIDTH, D_MODEL),
        "ffn2_norm": gain(ks[14], D_MODEL),
        "ffn2_w_gate": w(ks[15], D_MODEL, D_FF),
        "ffn2_w_up": w(ks[16], D_MODEL, D_FF),
        "ffn2_w_down": w(ks[17], D_FF, D_MODEL),
        "final_norm": 1.0 + 0.1 * jax.random.normal(ks[18], (D_MODEL,), f32),
    }


def reference(x, positions, ffn1_norm, ffn1_w_gate, ffn1_w_up, ffn1_w_down, mix_norm, w_in,
              mla_q_norm, mla_w_q_b, mla_kv_norm, mla_w_kv_b, swa_sinks, fox_forget_bias, w_out,
              ffn2_norm, ffn2_w_gate, ffn2_w_up, ffn2_w_down, final_norm):
    cos_m, sin_m = rope_tables(positions, MLA_ROPE_DIM)
    cos_s, sin_s = rope_tables(positions, SWA_HEAD_DIM)
    for l in range(DEPTH):
        x = x + 0.5 * swiglu(rmsnorm(x, ffn1_norm[l]), ffn1_w_gate[l], ffn1_w_up[l], ffn1_w_down[l])
        x = x + hybrid_mixer(rmsnorm(x, mix_norm[l]), cos_m, sin_m, cos_s, sin_s, w_in[l],
                             mla_q_norm[l], mla_w_q_b[l], mla_kv_norm[l], mla_w_kv_b[l],
                             swa_sinks[l], fox_forget_bias[l], w_out[l])
        x = x + 0.5 * swiglu(rmsnorm(x, ffn2_norm[l]), ffn2_w_gate[l], ffn2_w_up[l], ffn2_w_down[l])
    return rmsnorm(x, final_norm)
```

```python
import functools

import numpy as np
import jax
import jax.numpy as jnp
from jax import lax
from jax.experimental import pallas as pl
from jax.experimental.pallas import tpu as pltpu

RMS_EPS = 1e-6
ROPE_THETA = 10000.0

MLA_HEADS = 8
MLA_Q_LORA = 512
MLA_KV_LORA = 256
MLA_NOPE = 128
MLA_ROPE = 64
MLA_V = 128

SWA_HEADS = 8
SWA_KV_HEADS = 2
SWA_DIM = 64
WINDOW = 128

FOX_HEADS = 8
FOX_DIM = 64

LANES = 128
HALF = 32
VMEM_LIMIT = 56 * 1024 * 1024
NEG = -1e30

F32 = jnp.float32
BF16 = jnp.bfloat16

_O_CQ, _O_CKV, _O_KR, _O_QS, _O_KS, _O_VS, _O_QF, _O_KF, _O_VF, _O_FG, IN_COLS_R = (
    0, 512, 768, 896, 1408, 1664, 1920, 2432, 2944, 3456, 3584)


def _chunk_pair(base_a, base_b):
    r = np.arange(HALF)
    return np.concatenate([base_a + r, base_b + r, base_a + HALF + r, base_b + HALF + r])


def _in_perm():
    src = [0, 512, 768, 832, 1344, 1472, 1600, 2112, 2624, 3136]
    zero = 3144
    cols = [np.arange(src[0], src[0] + 512), np.arange(src[1], src[1] + 256)]
    cols.append(_chunk_pair(src[2], src[2]))
    for c in range(SWA_HEADS // 2):
        cols.append(_chunk_pair(src[3] + SWA_DIM * 2 * c, src[3] + SWA_DIM * (2 * c + 1)))
    for g in range(SWA_KV_HEADS):
        cols.append(_chunk_pair(src[4] + SWA_DIM * g, src[4] + SWA_DIM * g))
    for g in range(SWA_KV_HEADS):
        v = src[5] + SWA_DIM * g + np.arange(SWA_DIM)
        cols.append(np.concatenate([v, v]))
    cols += [np.arange(src[6], src[6] + 512), np.arange(src[7], src[7] + 512), np.arange(src[8], src[8] + 512)]
    cols.append(np.concatenate([np.arange(src[9], src[9] + FOX_HEADS), np.full(LANES - FOX_HEADS, zero)]))
    out = np.concatenate(cols).astype(np.int32)
    assert out.shape[0] == IN_COLS_R
    return out


def _qb_perm():
    per = MLA_NOPE + MLA_ROPE
    cols = [per * h + np.arange(MLA_NOPE) for h in range(MLA_HEADS)]
    for c in range(MLA_HEADS // 2):
        cols.append(_chunk_pair(per * 2 * c + MLA_NOPE, per * (2 * c + 1) + MLA_NOPE))
    return np.concatenate(cols).astype(np.int32)


def _kvb_perm():
    per = MLA_NOPE + MLA_V
    k = [per * h + np.arange(MLA_NOPE) for h in range(MLA_HEADS)]
    v = [per * h + MLA_NOPE + np.arange(MLA_V) for h in range(MLA_HEADS)]
    return np.concatenate(k + v).astype(np.int32)


_IN_PERM, _QB_PERM, _KVB_PERM = _in_perm(), _qb_perm(), _kvb_perm()


def _params(*sem):
    return pltpu.CompilerParams(dimension_semantics=sem, vmem_limit_bytes=VMEM_LIMIT)


def _resident(block_shape, index_map):
    return pl.BlockSpec(block_shape, index_map, pipeline_mode=pl.Buffered(1))


def _rms(x, gain):
    ms = jnp.mean(x * x, axis=-1, keepdims=True)
    return x * lax.rsqrt(ms + RMS_EPS) * gain


def _ffn_kernel(x_ref, g_ref, wg_ref, wu_ref, wd_ref, fg_ref, o_ref, h_sc, acc_sc, *, final_norm):
    f = pl.program_id(1)

    @pl.when(f == 0)
    def _():
        h_sc[...] = _rms(x_ref[...], g_ref[...]).astype(BF16)
        acc_sc[...] = jnp.zeros_like(acc_sc)

    h = h_sc[...]
    gate = jnp.dot(h, wg_ref[...], preferred_element_type=F32)
    up = jnp.dot(h, wu_ref[...], preferred_element_type=F32)
    act = (gate * (1.0 / (1.0 + jnp.exp(-gate))) * up).astype(BF16)
    acc_sc[...] += jnp.dot(act, wd_ref[...], preferred_element_type=F32)

    @pl.when(f == pl.num_programs(1) - 1)
    def _():
        y = x_ref[...] + 0.5 * acc_sc[...]
        if final_norm:
            y = _rms(y, fg_ref[...])
        o_ref[...] = y


def _ffn(x, gain, wg, wu, wd, layer, final_gain, *, final_norm, tm=512, tf=512):
    T, D = x.shape
    F = wg.shape[-1]
    assert T % tm == 0 and F % tf == 0
    return pl.pallas_call(
        functools.partial(_ffn_kernel, final_norm=final_norm),
        grid=(T // tm, F // tf),
        in_specs=[
            pl.BlockSpec((tm, D), lambda i, f: (i, 0)),
            pl.BlockSpec((None, 1, D), lambda i, f: (layer, 0, 0)),
            pl.BlockSpec((None, D, tf), lambda i, f: (layer, 0, f)),
            pl.BlockSpec((None, D, tf), lambda i, f: (layer, 0, f)),
            pl.BlockSpec((None, tf, D), lambda i, f: (layer, f, 0)),
            pl.BlockSpec((1, D), lambda i, f: (0, 0)),
        ],
        out_specs=pl.BlockSpec((tm, D), lambda i, f: (i, 0)),
        out_shape=jax.ShapeDtypeStruct((T, D), F32),
        scratch_shapes=[pltpu.VMEM((tm, D), BF16), pltpu.VMEM((tm, D), F32)],
        compiler_params=_params("parallel", "arbitrary"),
        name="ffn",
    )(x, gain, wg, wu, wd, final_gain)


def _rope(x, cos, sin_signed):
    return x * cos + pltpu.roll(x, 2 * HALF, axis=1) * sin_signed


def _mixin_kernel(x_ref, g_ref, win_ref, qn_ref, wqb_ref, kvn_ref, wkvb_ref, cos_ref, sin_ref, fb_ref,
                  qm_ref, km_ref, vm_ref, qs_ref, ks_ref, vs_ref, qf_ref, kf_ref, vf_ref, ccol_ref, crow_ref,
                  carry_sc, *, steps_per_seq):
    i = pl.program_id(0)
    tm = x_ref.shape[0]
    h = _rms(x_ref[...], g_ref[...]).astype(BF16)
    cos = cos_ref[...]
    sin = sin_ref[...]
    lane = lax.broadcasted_iota(jnp.int32, (1, LANES), 1)
    pair_lo = (lane % (2 * HALF)) < HALF
    half_lo = lane < (LANES // 2)

    def proj(a, b):
        return jnp.dot(h, win_ref[:, a:b], preferred_element_type=F32)

    cq = _rms(proj(_O_CQ, _O_CKV), qn_ref[...]).astype(BF16)
    q = jnp.dot(cq, wqb_ref[...], preferred_element_type=F32)
    q_scale = (MLA_NOPE + MLA_ROPE) ** -0.5
    n_nope = MLA_HEADS * MLA_NOPE
    for c in range(MLA_HEADS // 2):
        pe = (_rope(q[:, n_nope + LANES * c:n_nope + LANES * (c + 1)], cos, sin) * q_scale).astype(BF16)
        for hh in (2 * c, 2 * c + 1):
            qm_ref[:, 2 * LANES * hh:2 * LANES * hh + LANES] = (
                q[:, MLA_NOPE * hh:MLA_NOPE * (hh + 1)] * q_scale).astype(BF16)
            qm_ref[:, 2 * LANES * hh + LANES:2 * LANES * (hh + 1)] = pe

    ckv = _rms(proj(_O_CKV, _O_KR), kvn_ref[...]).astype(BF16)
    kv = jnp.dot(ckv, wkvb_ref[...], preferred_element_type=F32)
    kr = _rope(proj(_O_KR, _O_QS), cos, sin)
    kr_pair = (jnp.where(pair_lo, kr, 0.0).astype(BF16), jnp.where(pair_lo, 0.0, kr).astype(BF16))
    for hh in range(MLA_HEADS):
        km_ref[:, 2 * LANES * hh:2 * LANES * hh + LANES] = kv[:, MLA_NOPE * hh:MLA_NOPE * (hh + 1)].astype(BF16)
        km_ref[:, 2 * LANES * hh + LANES:2 * LANES * (hh + 1)] = kr_pair[hh % 2]
    vm_ref[...] = kv[:, n_nope:].astype(BF16)

    qs = proj(_O_QS, _O_KS)
    for c in range(SWA_HEADS // 2):
        qs_ref[:, LANES * c:LANES * (c + 1)] = (
            _rope(qs[:, LANES * c:LANES * (c + 1)], cos, sin) * SWA_DIM ** -0.5).astype(BF16)
    ks = proj(_O_KS, _O_VS)
    for g in range(SWA_KV_HEADS):
        ks_ref[:, LANES * g:LANES * (g + 1)] = _rope(ks[:, LANES * g:LANES * (g + 1)], cos, sin).astype(BF16)
    vs_ref[...] = proj(_O_VS, _O_QF).astype(BF16)

    qf_ref[...] = (proj(_O_QF, _O_KF) * FOX_DIM ** -0.5).astype(BF16)
    kf = proj(_O_KF, _O_VF)
    for hh in range(FOX_HEADS):
        kc = kf[:, LANES * (hh // 2):LANES * (hh // 2 + 1)]
        kf_ref[:, LANES * hh:LANES * (hh + 1)] = (
            jnp.where(half_lo, kc, 0.0) if hh % 2 == 0 else jnp.where(half_lo, 0.0, kc)).astype(BF16)
    vf_ref[...] = proj(_O_VF, _O_FG).astype(BF16)

    fl = proj(_O_FG, IN_COLS_R) + fb_ref[...]
    logf = jnp.minimum(fl, 0.0) - jnp.log(1.0 + jnp.exp(-jnp.abs(fl)))

    @pl.when(i % steps_per_seq == 0)
    def _():
        carry_sc[...] = jnp.zeros_like(carry_sc)

    row = lax.broadcasted_iota(jnp.int32, (tm, tm), 0)
    col = lax.broadcasted_iota(jnp.int32, (tm, tm), 1)
    tri = (col <= row).astype(BF16)
    a1 = logf.astype(BF16)
    r1 = logf - a1.astype(F32)
    a2 = r1.astype(BF16)
    a3 = (r1 - a2.astype(F32)).astype(BF16)
    c = (jnp.dot(tri, a1, preferred_element_type=F32) + jnp.dot(tri, a2, preferred_element_type=F32)
         + jnp.dot(tri, a3, preferred_element_type=F32)) + carry_sc[...]
    carry_sc[...] = c[tm - 1:tm, :]
    ccol_ref[...] = c
    crow_ref[...] = jnp.transpose(c)[:FOX_HEADS, :]


def _mixin(x, gain, w_in, qn, wqb, kvn, wkvb, cos, sin, fb, layer, *, batch, tm=256):
    T, D = x.shape
    S = T // batch
    assert S % tm == 0
    n = T // tm
    sps = S // tm

    def row(width):
        return pl.BlockSpec((tm, width), lambda i: (i, 0))

    def wres(a):
        return _resident((None,) + a.shape[1:], lambda i: (layer, 0, 0))

    widths = dict(qm=2 * LANES * MLA_HEADS, km=2 * LANES * MLA_HEADS, vm=MLA_V * MLA_HEADS,
                  qs=SWA_DIM * SWA_HEADS, ks=LANES * SWA_KV_HEADS, vs=LANES * SWA_KV_HEADS,
                  qf=FOX_DIM * FOX_HEADS, kf=LANES * FOX_HEADS, vf=FOX_DIM * FOX_HEADS)
    out_shape = [jax.ShapeDtypeStruct((T, w), BF16) for w in widths.values()]
    out_specs = [row(w) for w in widths.values()]
    out_shape += [jax.ShapeDtypeStruct((T, LANES), F32), jax.ShapeDtypeStruct((batch, FOX_HEADS, S), F32)]
    out_specs += [row(LANES), pl.BlockSpec((None, FOX_HEADS, tm), lambda i: (i // sps, 0, i % sps))]
    return pl.pallas_call(
        functools.partial(_mixin_kernel, steps_per_seq=sps),
        grid=(n,),
        in_specs=[row(D), wres(gain), wres(w_in), wres(qn), wres(wqb), wres(kvn), wres(wkvb),
                  row(LANES), row(LANES), wres(fb)],
        out_specs=out_specs,
        out_shape=out_shape,
        scratch_shapes=[pltpu.VMEM((1, LANES), F32)],
        compiler_params=_params("arbitrary"),
        name="mixin",
    )(x, gain, w_in, qn, wqb, kvn, wkvb, cos, sin, fb)


def _online_step(s, v, m, l, acc):
    m_new = jnp.maximum(m, jnp.max(s, axis=-1, keepdims=True))
    alpha = jnp.exp(m - m_new)
    p = jnp.exp(s - m_new)
    l = alpha * l + jnp.sum(p, axis=-1, keepdims=True)
    acc = alpha * acc + jnp.dot(p.astype(BF16), v, preferred_element_type=F32)
    return m_new, l, acc


def _qk(q, k):
    return lax.dot_general(q, k, (((1,), (1,)), ((), ())), preferred_element_type=F32)


def _causal_mask(t):
    row = lax.broadcasted_iota(jnp.int32, (t, t), 0)
    col = lax.broadcasted_iota(jnp.int32, (t, t), 1)
    return col <= row


def _mla_kernel(q_ref, k_ref, v_ref, o_ref):
    i = pl.program_id(2)
    t = q_ref.shape[0]
    q = q_ref[...]

    def block(j, carry, masked):
        k = k_ref[pl.ds(pl.multiple_of(j * t, t), t), :]
        v = v_ref[pl.ds(pl.multiple_of(j * t, t), t), :]
        s = _qk(q, k)
        if masked:
            s = jnp.where(_causal_mask(t), s, NEG)
        return _online_step(s, v, *carry)

    init = (jnp.full((t, 1), NEG, F32), jnp.zeros((t, 1), F32), jnp.zeros((t, v_ref.shape[1]), F32))
    carry = lax.fori_loop(0, i, lambda j, c: block(j, c, False), init)
    _, l, acc = block(i, carry, True)
    o_ref[...] = (acc / l).astype(o_ref.dtype)


def _mla_attention(q, k, v, *, t=512):
    B, S, _ = q.shape
    assert S % t == 0
    return pl.pallas_call(
        _mla_kernel,
        grid=(B, MLA_HEADS, S // t),
        in_specs=[
            pl.BlockSpec((None, t, 2 * LANES), lambda b, h, i: (b, i, h)),
            pl.BlockSpec((None, S, 2 * LANES), lambda b, h, i: (b, 0, h)),
            pl.BlockSpec((None, S, MLA_V), lambda b, h, i: (b, 0, h)),
        ],
        out_specs=pl.BlockSpec((None, t, MLA_V), lambda b, h, i: (b, i, h)),
        out_shape=jax.ShapeDtypeStruct((B, S, MLA_HEADS * MLA_V), BF16),
        compiler_params=_params("parallel", "parallel", "arbitrary"),
        name="mla_attn",
    )(q, k, v)


def _fox_kernel(q_ref, k_ref, v_ref, ccol_ref, crow_ref, o_ref):
    pair = pl.program_id(1)
    i = pl.program_id(2)
    t = q_ref.shape[0]
    q = q_ref[...]
    lane = lax.broadcasted_iota(jnp.int32, (1, LANES), 1)
    ccol = ccol_ref[...]
    cq = [jnp.sum(jnp.where(lane == 2 * pair + e, ccol, 0.0), axis=-1, keepdims=True) for e in range(2)]

    def block(j, carry, masked):
        start = pl.multiple_of(j * t, t)
        v = v_ref[pl.ds(start, t), :]
        out = []
        for e in range(2):
            k = k_ref[pl.ds(start, t), LANES * e:LANES * (e + 1)]
            ck = crow_ref[pl.ds(2 * pair + e, 1), pl.ds(start, t)]
            s = _qk(q, k) + (cq[e] - ck)
            if masked:
                s = jnp.where(_causal_mask(t), s, NEG)
            out.append(_online_step(s, v, *carry[e]))
        return tuple(out)

    one = (jnp.full((t, 1), NEG, F32), jnp.zeros((t, 1), F32), jnp.zeros((t, LANES), F32))
    carry = lax.fori_loop(0, i, lambda j, c: block(j, c, False), (one, one))
    (_, l0, a0), (_, l1, a1) = block(i, carry, True)
    o_ref[...] = jnp.where(lane < FOX_DIM, a0 / l0, a1 / l1).astype(o_ref.dtype)


def _fox_attention(q, k, v, ccol, crow, *, t=512):
    B, S, _ = q.shape
    assert S % t == 0
    return pl.pallas_call(
        _fox_kernel,
        grid=(B, FOX_HEADS // 2, S // t),
        in_specs=[
            pl.BlockSpec((None, t, LANES), lambda b, p, i: (b, i, p)),
            pl.BlockSpec((None, S, 2 * LANES), lambda b, p, i: (b, 0, p)),
            pl.BlockSpec((None, S, LANES), lambda b, p, i: (b, 0, p)),
            pl.BlockSpec((None, t, LANES), lambda b, p, i: (b, i, 0)),
            pl.BlockSpec((None, FOX_HEADS, S), lambda b, p, i: (b, 0, 0)),
        ],
        out_specs=pl.BlockSpec((None, t, LANES), lambda b, p, i: (b, i, p)),
        out_shape=jax.ShapeDtypeStruct((B, S, FOX_HEADS * FOX_DIM), BF16),
        compiler_params=_params("parallel", "parallel", "arbitrary"),
        name="fox_attn",
    )(q, k, v, ccol, crow)


def _swa_kernel(sink_ref, q_ref, kc_ref, kp_ref, vc_ref, vp_ref, o_ref):
    i = pl.program_id(1)
    t = q_ref.shape[0]
    lane = lax.broadcasted_iota(jnp.int32, (1, LANES), 1)
    pair_lo = (lane % (2 * HALF)) < HALF
    k_all = jnp.concatenate([kp_ref[...], kc_ref[...]], axis=0)
    v_all = jnp.concatenate([vp_ref[...], vc_ref[...]], axis=0)
    row = lax.broadcasted_iota(jnp.int32, (t, WINDOW + t), 0)
    col = lax.broadcasted_iota(jnp.int32, (t, WINDOW + t), 1)
    valid = (col <= row + WINDOW) & (col > row) & ((col >= WINDOW) | (i > 0))
    group = SWA_HEADS // SWA_KV_HEADS
    zero = jnp.zeros((), BF16)
    for c in range(SWA_HEADS // 2):
        g = (2 * c) // group
        q = q_ref[:, LANES * c:LANES * (c + 1)]
        kg = k_all[:, LANES * g:LANES * (g + 1)]
        vg = v_all[:, LANES * g:LANES * (g + 1)]
        outs = []
        for e in range(2):
            ke = jnp.where(pair_lo, kg, zero) if e == 0 else jnp.where(pair_lo, zero, kg)
            sink = sink_ref[2 * c + e]
            s = jnp.where(valid, _qk(q, ke), NEG)
            m = jnp.maximum(jnp.max(s, axis=-1, keepdims=True), sink)
            p = jnp.exp(s - m)
            l = jnp.sum(p, axis=-1, keepdims=True) + jnp.exp(sink - m)
            outs.append(jnp.dot(p.astype(BF16), vg, preferred_element_type=F32) / l)
        o_ref[:, LANES * c:LANES * (c + 1)] = jnp.where(lane < SWA_DIM, outs[0], outs[1]).astype(o_ref.dtype)


def _swa_attention(sinks, q, k, v, *, t=256):
    B, S, _ = q.shape
    assert S % t == 0 and t % WINDOW == 0
    r = t // WINDOW
    kv_w = LANES * SWA_KV_HEADS
    cur = pl.BlockSpec((None, t, kv_w), lambda b, i: (b, i, 0))
    prev = pl.BlockSpec((None, WINDOW, kv_w), lambda b, i: (b, jnp.maximum(i * r - 1, 0), 0))
    return pl.pallas_call(
        _swa_kernel,
        grid=(B, S // t),
        in_specs=[
            pl.BlockSpec(memory_space=pltpu.SMEM),
            pl.BlockSpec((None, t, SWA_HEADS * SWA_DIM), lambda b, i: (b, i, 0)),
            cur, prev, cur, prev,
        ],
        out_specs=pl.BlockSpec((None, t, SWA_HEADS * SWA_DIM), lambda b, i: (b, i, 0)),
        out_shape=jax.ShapeDtypeStruct((B, S, SWA_HEADS * SWA_DIM), BF16),
        compiler_params=_params("parallel", "arbitrary"),
        name="swa_attn",
    )(sinks, q, k, k, v, v)


def _outproj_kernel(x_ref, om_ref, os_ref, of_ref, w_ref, o_ref):
    a, b = om_ref.shape[1], om_ref.shape[1] + os_ref.shape[1]
    y = jnp.dot(om_ref[...], w_ref[:a, :], preferred_element_type=F32)
    y += jnp.dot(os_ref[...], w_ref[a:b, :], preferred_element_type=F32)
    y += jnp.dot(of_ref[...], w_ref[b:, :], preferred_element_type=F32)
    o_ref[...] = x_ref[...] + y


def _outproj(x, om, osw, of, w_out, layer, *, tm=512):
    T, D = x.shape
    assert T % tm == 0

    def row(width):
        return pl.BlockSpec((tm, width), lambda i: (i, 0))

    return pl.pallas_call(
        _outproj_kernel,
        grid=(T // tm,),
        in_specs=[row(D), row(om.shape[1]), row(osw.shape[1]), row(of.shape[1]),
                  _resident((None,) + w_out.shape[1:], lambda i: (layer, 0, 0))],
        out_specs=row(D),
        out_shape=jax.ShapeDtypeStruct((T, D), F32),
        compiler_params=_params("parallel"),
        name="outproj",
    )(x, om, osw, of, w_out)


def kernel(x, positions, ffn1_norm, ffn1_w_gate, ffn1_w_up, ffn1_w_down, mix_norm, w_in, mla_q_norm, mla_w_q_b, mla_kv_norm, mla_w_kv_b, swa_sinks, fox_forget_bias, w_out, ffn2_norm, ffn2_w_gate, ffn2_w_up, ffn2_w_down, final_norm):
    B, S, D = x.shape
    depth = w_in.shape[0]
    T = B * S

    bf = lambda w: w.astype(BF16)
    w_in_r = bf(jnp.take(jnp.pad(w_in, ((0, 0), (0, 0), (0, 1))), _IN_PERM, axis=2))
    wqb_r = bf(jnp.take(mla_w_q_b, _QB_PERM, axis=2))
    wkvb_r = bf(jnp.take(mla_w_kv_b, _KVB_PERM, axis=2))
    ffn_w = [(bf(ffn1_w_gate), bf(ffn1_w_up), bf(ffn1_w_down)), (bf(ffn2_w_gate), bf(ffn2_w_up), bf(ffn2_w_down))]
    ffn_g = [ffn1_norm[:, None, :], ffn2_norm[:, None, :]]
    w_out_b = bf(w_out)
    mix_g = mix_norm[:, None, :]
    qn = mla_q_norm[:, None, :]
    kvn = mla_kv_norm[:, None, :]
    fb = jnp.pad(fox_forget_bias, ((0, 0), (0, LANES - FOX_HEADS)))[:, None, :]
    fin_g = final_norm[None, :]

    inv_freq = ROPE_THETA ** (-jnp.arange(0, 2 * HALF, 2, dtype=F32) / (2 * HALF))
    ang = positions.astype(F32).reshape(T, 1) * inv_freq
    cos, sin = jnp.cos(ang), jnp.sin(ang)
    cos_t = jnp.concatenate([cos, cos, cos, cos], axis=-1)
    sin_t = jnp.concatenate([-sin, -sin, sin, sin], axis=-1)

    xt = x.reshape(T, D)
    for l in range(depth):
        xt = _ffn(xt, ffn_g[0], *ffn_w[0], l, fin_g, final_norm=False)
        qm, km, vm, qs, ks, vs, qf, kf, vf, ccol, crow = _mixin(
            xt, mix_g, w_in_r, qn, wqb_r, kvn, wkvb_r, cos_t, sin_t, fb, l, batch=B)
        sh = lambda a: a.reshape(B, S, a.shape[-1])
        o_mla = _mla_attention(sh(qm), sh(km), sh(vm))
        o_swa = _swa_attention(swa_sinks[l], sh(qs), sh(ks), sh(vs))
        o_fox = _fox_attention(sh(qf), sh(kf), sh(vf), sh(ccol), crow)
        xt = _outproj(xt, o_mla.reshape(T, -1), o_swa.reshape(T, -1), o_fox.reshape(T, -1), w_out_b, l)
        xt = _ffn(xt, ffn_g[1], *ffn_w[1], l, fin_g, final_norm=(l == depth - 1))
    return xt.reshape(B, S, D)
```

```python
import functools
import math

import numpy as np
import jax
import jax.numpy as jnp
from jax import lax
from jax.experimental import pallas as pl
from jax.experimental.pallas import tpu as pltpu

RMS_EPS = 1e-6
ROPE_THETA = 10000.0

MLA_HEADS = 8
MLA_Q_LORA = 512
MLA_KV_LORA = 256
MLA_NOPE = 128
MLA_ROPE = 64
MLA_V = 128

SWA_HEADS = 8
SWA_KV_HEADS = 2
SWA_DIM = 64
WINDOW = 128

FOX_HEADS = 8
FOX_DIM = 64

LANES = 128
HALF = 32
VMEM_LIMIT = 56 * 1024 * 1024
FLASH_VMEM_LIMIT = 60 * 1024 * 1024
FLASH_LAG = 2
FLASH_BUFS = 2 * FLASH_LAG
NEG = -1e30
LOG2E = math.log2(math.e)
GATE_COPIES = 6

F32 = jnp.float32
BF16 = jnp.bfloat16

_O_CQ, _O_CKV, _O_KR, _O_QS, _O_KS, _O_VS, _O_QF, _O_KF, _O_VF, _O_FG, IN_COLS_R = (
    0, 512, 768, 896, 1408, 1664, 1920, 2432, 2944, 3456, 3584)


def _chunk_pair(base_a, base_b):
    r = np.arange(HALF)
    return np.concatenate([base_a + r, base_b + r, base_a + HALF + r, base_b + HALF + r])


def _in_perm():
    src = [0, 512, 768, 832, 1344, 1472, 1600, 2112, 2624, 3136]
    zero = 3144
    cols = [np.arange(src[0], src[0] + 512), np.arange(src[1], src[1] + 256)]
    cols.append(_chunk_pair(src[2], src[2]))
    for c in range(SWA_HEADS // 2):
        cols.append(_chunk_pair(src[3] + SWA_DIM * 2 * c, src[3] + SWA_DIM * (2 * c + 1)))
    for g in range(SWA_KV_HEADS):
        cols.append(_chunk_pair(src[4] + SWA_DIM * g, src[4] + SWA_DIM * g))
    for g in range(SWA_KV_HEADS):
        v = src[5] + SWA_DIM * g + np.arange(SWA_DIM)
        cols.append(np.concatenate([v, v]))
    cols += [np.arange(src[6], src[6] + 512), np.arange(src[7], src[7] + 512), np.arange(src[8], src[8] + 512)]
    gate = np.tile(np.arange(src[9], src[9] + FOX_HEADS), GATE_COPIES)
    cols.append(np.concatenate([gate, np.full(LANES - gate.shape[0], zero)]))
    out = np.concatenate(cols).astype(np.int32)
    assert out.shape[0] == IN_COLS_R
    return out


def _qb_perm():
    per = MLA_NOPE + MLA_ROPE
    cols = [per * h + np.arange(MLA_NOPE) for h in range(MLA_HEADS)]
    for c in range(MLA_HEADS // 2):
        cols.append(_chunk_pair(per * 2 * c + MLA_NOPE, per * (2 * c + 1) + MLA_NOPE))
    return np.concatenate(cols).astype(np.int32)


def _kvb_perm():
    per = MLA_NOPE + MLA_V
    k = [per * h + np.arange(MLA_NOPE) for h in range(MLA_HEADS)]
    v = [per * h + MLA_NOPE + np.arange(MLA_V) for h in range(MLA_HEADS)]
    return np.concatenate(k + v).astype(np.int32)


_IN_PERM, _QB_PERM, _KVB_PERM = _in_perm(), _qb_perm(), _kvb_perm()


def _params(*sem):
    return pltpu.CompilerParams(dimension_semantics=sem, vmem_limit_bytes=VMEM_LIMIT)


def _resident(block_shape, index_map):
    return pl.BlockSpec(block_shape, index_map, pipeline_mode=pl.Buffered(1))


def _rms(x, gain):
    ms = jnp.mean(x * x, axis=-1, keepdims=True)
    return x * lax.rsqrt(ms + RMS_EPS) * gain


def _ffn_kernel(x_ref, g_ref, wg_ref, wu_ref, wd_ref, fg_ref, o_ref, h_sc, acc_sc, *, final_norm):
    f = pl.program_id(1)

    @pl.when(f == 0)
    def _():
        h_sc[...] = _rms(x_ref[...], g_ref[...]).astype(BF16)
        acc_sc[...] = jnp.zeros_like(acc_sc)

    h = h_sc[...]
    gate = jnp.dot(h, wg_ref[...], preferred_element_type=F32)
    up = jnp.dot(h, wu_ref[...], preferred_element_type=F32)
    act = (gate * (1.0 / (1.0 + jnp.exp(-gate))) * up).astype(BF16)
    acc_sc[...] += jnp.dot(act, wd_ref[...], preferred_element_type=F32)

    @pl.when(f == pl.num_programs(1) - 1)
    def _():
        y = x_ref[...] + 0.5 * acc_sc[...]
        if final_norm:
            y = _rms(y, fg_ref[...])
        o_ref[...] = y


def _ffn(x, gain, wg, wu, wd, layer, final_gain, *, final_norm, tm=512, tf=512):
    T, D = x.shape
    F = wg.shape[-1]
    assert T % tm == 0 and F % tf == 0
    return pl.pallas_call(
        functools.partial(_ffn_kernel, final_norm=final_norm),
        grid=(T // tm, F // tf),
        in_specs=[
            pl.BlockSpec((tm, D), lambda i, f: (i, 0)),
            pl.BlockSpec((None, 1, D), lambda i, f: (layer, 0, 0)),
            pl.BlockSpec((None, D, tf), lambda i, f: (layer, 0, f)),
            pl.BlockSpec((None, D, tf), lambda i, f: (layer, 0, f)),
            pl.BlockSpec((None, tf, D), lambda i, f: (layer, f, 0)),
            pl.BlockSpec((1, D), lambda i, f: (0, 0)),
        ],
        out_specs=pl.BlockSpec((tm, D), lambda i, f: (i, 0)),
        out_shape=jax.ShapeDtypeStruct((T, D), F32),
        scratch_shapes=[pltpu.VMEM((tm, D), BF16), pltpu.VMEM((tm, D), F32)],
        compiler_params=_params("parallel", "arbitrary"),
        name="ffn",
    )(x, gain, wg, wu, wd, final_gain)


def _rope(x, cos, sin_signed):
    return x * cos + pltpu.roll(x, 2 * HALF, axis=1) * sin_signed


def _split3(x):
    a1 = x.astype(BF16)
    r1 = x - a1.astype(F32)
    a2 = r1.astype(BF16)
    a3 = (r1 - a2.astype(F32)).astype(BF16)
    return a1, a2, a3


def _mixin_kernel(x_ref, g_ref, win_ref, qn_ref, wqb_ref, kvn_ref, wkvb_ref, cos_ref, sin_ref, fb_ref,
                  qmn_ref, qmp_ref, kmn_ref, kmr_ref, vm_ref, qs_ref, ks_ref, vs_ref,
                  qf_ref, kf_ref, vf_ref, qaug_ref, kaug_ref, carry_sc, *, steps_per_seq):
    i = pl.program_id(0)
    tm = x_ref.shape[0]
    h = _rms(x_ref[...], g_ref[...]).astype(BF16)
    cos = cos_ref[...]
    sin = sin_ref[...]
    lane = lax.broadcasted_iota(jnp.int32, (1, LANES), 1)

    def proj(a, b):
        return jnp.dot(h, win_ref[:, a:b], preferred_element_type=F32)

    cq = _rms(proj(_O_CQ, _O_CKV), qn_ref[...]).astype(BF16)
    q = jnp.dot(cq, wqb_ref[...], preferred_element_type=F32)
    q_scale = (MLA_NOPE + MLA_ROPE) ** -0.5 * LOG2E
    n_nope = MLA_HEADS * MLA_NOPE
    qmn_ref[...] = (q[:, :n_nope] * q_scale).astype(BF16)
    for c in range(MLA_HEADS // 2):
        sl = slice(LANES * c, LANES * (c + 1))
        qmp_ref[:, sl] = (_rope(q[:, n_nope + LANES * c:n_nope + LANES * (c + 1)], cos, sin) * q_scale).astype(BF16)

    ckv = _rms(proj(_O_CKV, _O_KR), kvn_ref[...]).astype(BF16)
    kv = jnp.dot(ckv, wkvb_ref[...], preferred_element_type=F32)
    kmn_ref[...] = kv[:, :n_nope].astype(BF16)
    vm_ref[...] = kv[:, n_nope:].astype(BF16)
    kmr_ref[...] = _rope(proj(_O_KR, _O_QS), cos, sin).astype(BF16)

    qs = proj(_O_QS, _O_KS)
    for c in range(SWA_HEADS // 2):
        sl = slice(LANES * c, LANES * (c + 1))
        qs_ref[:, sl] = (_rope(qs[:, sl], cos, sin) * SWA_DIM ** -0.5).astype(BF16)
    ks = proj(_O_KS, _O_VS)
    for g in range(SWA_KV_HEADS):
        sl = slice(LANES * g, LANES * (g + 1))
        ks_ref[:, sl] = _rope(ks[:, sl], cos, sin).astype(BF16)
    vs_ref[...] = proj(_O_VS, _O_QF).astype(BF16)

    qf_ref[...] = (proj(_O_QF, _O_KF) * (FOX_DIM ** -0.5 * LOG2E)).astype(BF16)
    kf_ref[...] = proj(_O_KF, _O_VF).astype(BF16)
    vf_ref[...] = proj(_O_VF, _O_FG).astype(BF16)

    fl = proj(_O_FG, IN_COLS_R) + fb_ref[...]
    logf = jnp.minimum(fl, 0.0) - jnp.log(1.0 + jnp.exp(-jnp.abs(fl)))

    @pl.when(i % steps_per_seq == 0)
    def _():
        carry_sc[...] = jnp.zeros_like(carry_sc)

    row = lax.broadcasted_iota(jnp.int32, (tm, tm), 0)
    col = lax.broadcasted_iota(jnp.int32, (tm, tm), 1)
    tri = (col <= row).astype(BF16)
    c = carry_sc[...]
    for term in _split3(logf):
        c = c + jnp.dot(tri, term, preferred_element_type=F32)
    carry_sc[...] = c[tm - 1:tm, :]

    c1, c2, c3 = (term.astype(F32) for term in _split3(c * LOG2E))
    g8 = lane // FOX_HEADS
    ones = jnp.where(g8 < GATE_COPIES, 1.0, 0.0)
    qaug_ref[...] = jnp.where(g8 < 3, ones, jnp.where(g8 == 3, c1, jnp.where(g8 == 4, c2, jnp.where(g8 == 5, c3, 0.0)))).astype(BF16)
    kaug_ref[...] = jnp.where(g8 == 0, -c1, jnp.where(g8 == 1, -c2, jnp.where(g8 == 2, -c3, ones))).astype(BF16)


def _mixin(x, gain, w_in, qn, wqb, kvn, wkvb, cos, sin, fb, layer, *, batch, tm=256):
    T, D = x.shape
    S = T // batch
    assert S % tm == 0
    sps = S // tm

    def row(width):
        return pl.BlockSpec((tm, width), lambda i: (i, 0))

    def wres(a):
        return _resident((None,) + a.shape[1:], lambda i: (layer, 0, 0))

    widths = dict(qmn=MLA_NOPE * MLA_HEADS, qmp=MLA_ROPE * MLA_HEADS, kmn=MLA_NOPE * MLA_HEADS, kmr=LANES,
                  vm=MLA_V * MLA_HEADS, qs=SWA_DIM * SWA_HEADS, ks=LANES * SWA_KV_HEADS, vs=LANES * SWA_KV_HEADS,
                  qf=FOX_DIM * FOX_HEADS, kf=FOX_DIM * FOX_HEADS, vf=FOX_DIM * FOX_HEADS, qaug=LANES, kaug=LANES)
    return pl.pallas_call(
        functools.partial(_mixin_kernel, steps_per_seq=sps),
        grid=(T // tm,),
        in_specs=[row(D), wres(gain), wres(w_in), wres(qn), wres(wqb), wres(kvn), wres(wkvb),
                  row(LANES), row(LANES), wres(fb)],
        out_specs=[row(w) for w in widths.values()],
        out_shape=[jax.ShapeDtypeStruct((T, w), BF16) for w in widths.values()],
        scratch_shapes=[pltpu.VMEM((1, LANES), F32)],
        compiler_params=_params("arbitrary"),
        name="mixin",
    )(x, gain, w_in, qn, wqb, kvn, wkvb, cos, sin, fb)


def _qk(q, k):
    return lax.dot_general(q, k, (((1,), (1,)), ((), ())), preferred_element_type=F32)


def _causal_mask(t):
    row = lax.broadcasted_iota(jnp.int32, (t, t), 0)
    col = lax.broadcasted_iota(jnp.int32, (t, t), 1)
    return col <= row


def _flash_kernel(tab_ref, qa_ref, qb_ref, ka_ref, kb_ref, v_ref, o_ref,
                  q_sc, k_sc, v_sc, m_sc, acc_sc, *bufs, kind, t, n_off):
    h = pl.program_id(1)
    n = q_sc.shape[0] // t
    lane = lax.broadcasted_iota(jnp.int32, (1, LANES), 1)
    even = (h % 2) == 0
    zero = jnp.zeros((), BF16)

    if kind == "mla":
        mine = ((lane % (2 * HALF)) < HALF) == even
        q_sc[:, :LANES] = qa_ref[...]
        q_sc[:, LANES:] = qb_ref[...]
        k_sc[:, :LANES] = ka_ref[...]
        k_sc[:, LANES:] = jnp.where(mine, kb_ref[...], zero)
        v_sc[:, :LANES] = v_ref[...]
    else:
        mine = (lane < FOX_DIM) == even
        gate = ((lane % FOX_HEADS) == h) & (lane < GATE_COPIES * FOX_HEADS)
        q_sc[:, :LANES] = qa_ref[...]
        q_sc[:, LANES:] = jnp.where(gate, qb_ref[...], zero)
        k_sc[:, :LANES] = jnp.where(mine, ka_ref[...], zero)
        k_sc[:, LANES:] = kb_ref[...]
        v_sc[:, :LANES] = jnp.where(mine, v_ref[...], zero)
    v_sc[:, LANES:] = jnp.ones((v_sc.shape[0], LANES), BF16)

    def rows(j):
        return pl.ds(pl.multiple_of(j * t, t), t)

    s_bufs, p_bufs, a_bufs = bufs[:FLASH_BUFS], bufs[FLASH_BUFS:2 * FLASH_BUFS], bufs[2 * FLASH_BUFS:]

    def scores(i, j, dst):
        dst[...] = _qk(q_sc[rows(i), :], k_sc[rows(j), :])

    def softmax(par, i, first):
        s = s_bufs[par][...]
        if first:
            s = jnp.where(_causal_mask(t), s, NEG)
            m_new = jnp.max(s, axis=-1, keepdims=True)
        else:
            m_old = m_sc[rows(i), :]
            m_new = jnp.maximum(m_old, jnp.max(s, axis=-1, keepdims=True))
            a_bufs[par][...] = jnp.exp2(m_old - m_new)
        p_bufs[par][...] = jnp.exp2(s - m_new).astype(BF16)
        m_sc[rows(i), :] = m_new

    def accumulate(par, i, j, first):
        pv = jnp.dot(p_bufs[par][...], v_sc[rows(j), :], preferred_element_type=F32)
        if first:
            acc_sc[rows(i), :] = pv
        else:
            acc_sc[rows(i), :] = a_bufs[par][...] * acc_sc[rows(i), :] + pv

    def pipeline(base, count, first, lag):
        nb = 2 * lag
        blk = lambda a: (tab_ref[0, base + a], tab_ref[1, base + a])

        def step(a, u, with_acc=True):
            if with_acc:
                accumulate((u - lag) % nb, *blk(a - lag), first)
            scores(*blk(a + lag), s_bufs[(u + lag) % nb])
            softmax(u, blk(a)[0], first)

        for a in range(lag):
            scores(*blk(a), s_bufs[a])
        for a in range(min(lag, count)):
            step(a, a, with_acc=False)
        main = max(count - lag, 0)

        def body(k, carry):
            for u in range(nb):
                step(lag + nb * k + u, (lag + u) % nb)
            return carry

        lax.fori_loop(0, main // nb, body, 0)
        for a in range(lag + main // nb * nb, count):
            step(a, a % nb)
        for a in range(main, count):
            accumulate(a % nb, *blk(a), first)

    pipeline(0, n, True, 1)
    pipeline(n, n_off, False, FLASH_LAG)

    def finish(i, carry):
        o_ref[rows(i), :] = (acc_sc[rows(i), :LANES] / acc_sc[rows(i), LANES:]).astype(o_ref.dtype)
        return carry

    lax.fori_loop(0, n, finish, 0)


def _flash_attention(qa, qb, ka, kb, v, *, kind, t=512):
    B, S, _ = qa.shape
    heads = MLA_HEADS if kind == "mla" else FOX_HEADS
    n = S // t
    assert S % t == 0 and n % 2 == 0
    diag = [(i, i) for i in range(n)]
    off = [(i, j) for j in range(n - 1) for i in range(j + 1, n)]
    table = jnp.asarray(np.array(diag + off + off[-1:] * (2 * FLASH_LAG), np.int32).T)
    if kind == "mla":
        a_map = lambda b, h: (b, 0, h)
        qb_map = lambda b, h: (b, 0, h // 2)
    else:
        a_map = lambda b, h: (b, 0, h // 2)
        qb_map = lambda b, h: (b, 0, 0)
    kb_map = lambda b, h: (b, 0, 0)
    full = lambda index_map: pl.BlockSpec((None, S, LANES), index_map)
    once = lambda index_map: pl.BlockSpec((None, S, LANES), index_map, pipeline_mode=pl.Buffered(1))
    return pl.pallas_call(
        functools.partial(_flash_kernel, kind=kind, t=t, n_off=len(off)),
        grid=(B, heads),
        in_specs=[pl.BlockSpec(memory_space=pltpu.SMEM),
                  once(a_map), once(qb_map), full(a_map), full(kb_map), full(a_map)],
        out_specs=pl.BlockSpec((None, S, LANES), lambda b, h: (b, 0, h)),
        out_shape=jax.ShapeDtypeStruct((B, S, heads * LANES), BF16),
        scratch_shapes=[pltpu.VMEM((S, 2 * LANES), BF16), pltpu.VMEM((S, 2 * LANES), BF16),
                        pltpu.VMEM((S, 2 * LANES), BF16),
                        pltpu.VMEM((S, 1), F32), pltpu.VMEM((S, 2 * LANES), F32)]
                       + [pltpu.VMEM((t, t), F32)] * FLASH_BUFS
                       + [pltpu.VMEM((t, t), BF16)] * FLASH_BUFS
                       + [pltpu.VMEM((t, 1), F32)] * FLASH_BUFS,
        compiler_params=pltpu.CompilerParams(dimension_semantics=("parallel", "arbitrary"),
                                             vmem_limit_bytes=FLASH_VMEM_LIMIT),
        name=kind + "_attn",
    )(table, qa, qb, ka, kb, v)


def _swa_kernel(sink_ref, q_ref, kc_ref, kp_ref, vc_ref, vp_ref, o_ref):
    i = pl.program_id(1)
    t = q_ref.shape[0]
    lane = lax.broadcasted_iota(jnp.int32, (1, LANES), 1)
    pair_lo = (lane % (2 * HALF)) < HALF
    k_all = jnp.concatenate([kp_ref[...], kc_ref[...]], axis=0)
    v_all = jnp.concatenate([vp_ref[...], vc_ref[...]], axis=0)
    row = lax.broadcasted_iota(jnp.int32, (t, WINDOW + t), 0)
    col = lax.broadcasted_iota(jnp.int32, (t, WINDOW + t), 1)
    valid = (col <= row + WINDOW) & (col > row) & ((col >= WINDOW) | (i > 0))
    group = SWA_HEADS // SWA_KV_HEADS
    zero = jnp.zeros((), BF16)
    for c in range(SWA_HEADS // 2):
        g = (2 * c) // group
        q = q_ref[:, LANES * c:LANES * (c + 1)]
        kg = k_all[:, LANES * g:LANES * (g + 1)]
        vg = v_all[:, LANES * g:LANES * (g + 1)]
        outs = []
        for e in range(2):
            ke = jnp.where(pair_lo, kg, zero) if e == 0 else jnp.where(pair_lo, zero, kg)
            sink = sink_ref[2 * c + e]
            s = jnp.where(valid, _qk(q, ke), NEG)
            m = jnp.maximum(jnp.max(s, axis=-1, keepdims=True), sink)
            p = jnp.exp(s - m)
            l = jnp.sum(p, axis=-1, keepdims=True) + jnp.exp(sink - m)
            outs.append(jnp.dot(p.astype(BF16), vg, preferred_element_type=F32) / l)
        o_ref[:, LANES * c:LANES * (c + 1)] = jnp.where(lane < SWA_DIM, outs[0], outs[1]).astype(o_ref.dtype)


def _swa_attention(sinks, q, k, v, *, t=256):
    B, S, _ = q.shape
    assert S % t == 0 and t % WINDOW == 0
    r = t // WINDOW
    kv_w = LANES * SWA_KV_HEADS
    cur = pl.BlockSpec((None, t, kv_w), lambda b, i: (b, i, 0))
    prev = pl.BlockSpec((None, WINDOW, kv_w), lambda b, i: (b, jnp.maximum(i * r - 1, 0), 0))
    return pl.pallas_call(
        _swa_kernel,
        grid=(B, S // t),
        in_specs=[
            pl.BlockSpec(memory_space=pltpu.SMEM),
            pl.BlockSpec((None, t, SWA_HEADS * SWA_DIM), lambda b, i: (b, i, 0)),
            cur, prev, cur, prev,
        ],
        out_specs=pl.BlockSpec((None, t, SWA_HEADS * SWA_DIM), lambda b, i: (b, i, 0)),
        out_shape=jax.ShapeDtypeStruct((B, S, SWA_HEADS * SWA_DIM), BF16),
        compiler_params=_params("parallel", "arbitrary"),
        name="swa_attn",
    )(sinks, q, k, k, v, v)


def _outproj_kernel(x_ref, om_ref, os_ref, of_ref, w_ref, o_ref):
    a, b = om_ref.shape[1], om_ref.shape[1] + os_ref.shape[1]
    y = jnp.dot(om_ref[...], w_ref[:a, :], preferred_element_type=F32)
    y += jnp.dot(os_ref[...], w_ref[a:b, :], preferred_element_type=F32)
    of = jnp.concatenate([of_ref[:, 2 * LANES * c:2 * LANES * c + LANES] + of_ref[:, 2 * LANES * c + LANES:2 * LANES * (c + 1)]
                          for c in range(FOX_HEADS // 2)], axis=1)
    y += jnp.dot(of, w_ref[b:, :], preferred_element_type=F32)
    o_ref[...] = x_ref[...] + y


def _outproj(x, om, osw, of, w_out, layer, *, tm=512):
    T, D = x.shape
    assert T % tm == 0

    def row(width):
        return pl.BlockSpec((tm, width), lambda i: (i, 0))

    return pl.pallas_call(
        _outproj_kernel,
        grid=(T // tm,),
        in_specs=[row(D), row(om.shape[1]), row(osw.shape[1]), row(of.shape[1]),
                  _resident((None,) + w_out.shape[1:], lambda i: (layer, 0, 0))],
        out_specs=row(D),
        out_shape=jax.ShapeDtypeStruct((T, D), F32),
        compiler_params=_params("parallel"),
        name="outproj",
    )(x, om, osw, of, w_out)


def kernel(x, positions, ffn1_norm, ffn1_w_gate, ffn1_w_up, ffn1_w_down, mix_norm, w_in, mla_q_norm, mla_w_q_b, mla_kv_norm, mla_w_kv_b, swa_sinks, fox_forget_bias, w_out, ffn2_norm, ffn2_w_gate, ffn2_w_up, ffn2_w_down, final_norm):
    B, S, D = x.shape
    depth = w_in.shape[0]
    T = B * S

    bf = lambda w: w.astype(BF16)
    w_in_r = bf(jnp.take(jnp.pad(w_in, ((0, 0), (0, 0), (0, 1))), _IN_PERM, axis=2))
    wqb_r = bf(jnp.take(mla_w_q_b, _QB_PERM, axis=2))
    wkvb_r = bf(jnp.take(mla_w_kv_b, _KVB_PERM, axis=2))
    ffn_w = [(bf(ffn1_w_gate), bf(ffn1_w_up), bf(ffn1_w_down)), (bf(ffn2_w_gate), bf(ffn2_w_up), bf(ffn2_w_down))]
    ffn_g = [ffn1_norm[:, None, :], ffn2_norm[:, None, :]]
    w_out_b = bf(w_out)
    mix_g = mix_norm[:, None, :]
    qn = mla_q_norm[:, None, :]
    kvn = mla_kv_norm[:, None, :]
    fb = jnp.pad(jnp.tile(fox_forget_bias, (1, GATE_COPIES)), ((0, 0), (0, LANES - GATE_COPIES * FOX_HEADS)))[:, None, :]
    fin_g = final_norm[None, :]

    inv_freq = ROPE_THETA ** (-jnp.arange(0, 2 * HALF, 2, dtype=F32) / (2 * HALF))
    ang = positions.astype(F32).reshape(T, 1) * inv_freq
    cos, sin = jnp.cos(ang), jnp.sin(ang)
    cos_t = jnp.concatenate([cos, cos, cos, cos], axis=-1)
    sin_t = jnp.concatenate([-sin, -sin, sin, sin], axis=-1)

    xt = x.reshape(T, D)
    sh = lambda a: a.reshape(B, S, a.shape[-1])
    for l in range(depth):
        xt = _ffn(xt, ffn_g[0], *ffn_w[0], l, fin_g, final_norm=False)
        qmn, qmp, kmn, kmr, vm, qs, ks, vs, qf, kf, vf, qaug, kaug = _mixin(
            xt, mix_g, w_in_r, qn, wqb_r, kvn, wkvb_r, cos_t, sin_t, fb, l, batch=B)
        o_mla = _flash_attention(sh(qmn), sh(qmp), sh(kmn), sh(kmr), sh(vm), kind="mla")
        o_swa = _swa_attention(swa_sinks[l], sh(qs), sh(ks), sh(vs))
        o_fox = _flash_attention(sh(qf), sh(qaug), sh(kf), sh(kaug), sh(vf), kind="fox")
        xt = _outproj(xt, o_mla.reshape(T, -1), o_swa.reshape(T, -1), o_fox.reshape(T, -1), w_out_b, l)
        xt = _ffn(xt, ffn_g[1], *ffn_w[1], l, fin_g, final_norm=(l == depth - 1))
    return xt.reshape(B, S, D)
```

```python
import functools
import math

import numpy as np
import jax
import jax.numpy as jnp
from jax import lax
from jax.experimental import pallas as pl
from jax.experimental.pallas import tpu as pltpu

RMS_EPS = 1e-6
ROPE_THETA = 10000.0

MLA_HEADS = 8
MLA_Q_LORA = 512
MLA_KV_LORA = 256
MLA_NOPE = 128
MLA_ROPE = 64
MLA_V = 128

SWA_HEADS = 8
SWA_KV_HEADS = 2
SWA_DIM = 64
WINDOW = 128

FOX_HEADS = 8
FOX_DIM = 64

LANES = 128
HALF = 32
VMEM_LIMIT = 56 * 1024 * 1024
FLASH_VMEM_LIMIT = 60 * 1024 * 1024
FLASH_LAG = 2
FLASH_BUFS = 2 * FLASH_LAG
NEG = -1e30
LOG2E = math.log2(math.e)
GATE_COPIES = 6

F32 = jnp.float32
BF16 = jnp.bfloat16

_O_CQ, _O_CKV, _O_KR, _O_QS, _O_KS, _O_VS, _O_QF, _O_KF, _O_VF, _O_FG, IN_COLS_R = (
    0, 512, 768, 896, 1408, 1664, 1920, 2432, 2944, 3456, 3584)


def _chunk_pair(base_a, base_b):
    r = np.arange(HALF)
    return np.concatenate([base_a + r, base_b + r, base_a + HALF + r, base_b + HALF + r])


def _in_perm():
    src = [0, 512, 768, 832, 1344, 1472, 1600, 2112, 2624, 3136]
    zero = 3144
    cols = [np.arange(src[0], src[0] + 512), np.arange(src[1], src[1] + 256)]
    cols.append(_chunk_pair(src[2], src[2]))
    for c in range(SWA_HEADS // 2):
        cols.append(_chunk_pair(src[3] + SWA_DIM * 2 * c, src[3] + SWA_DIM * (2 * c + 1)))
    for g in range(SWA_KV_HEADS):
        cols.append(_chunk_pair(src[4] + SWA_DIM * g, src[4] + SWA_DIM * g))
    for g in range(SWA_KV_HEADS):
        v = src[5] + SWA_DIM * g + np.arange(SWA_DIM)
        cols.append(np.concatenate([v, v]))
    cols += [np.arange(src[6], src[6] + 512), np.arange(src[7], src[7] + 512), np.arange(src[8], src[8] + 512)]
    gate = np.tile(np.arange(src[9], src[9] + FOX_HEADS), GATE_COPIES)
    cols.append(np.concatenate([gate, np.full(LANES - gate.shape[0], zero)]))
    out = np.concatenate(cols).astype(np.int32)
    assert out.shape[0] == IN_COLS_R
    return out


def _qb_perm():
    per = MLA_NOPE + MLA_ROPE
    cols = [per * h + np.arange(MLA_NOPE) for h in range(MLA_HEADS)]
    for c in range(MLA_HEADS // 2):
        cols.append(_chunk_pair(per * 2 * c + MLA_NOPE, per * (2 * c + 1) + MLA_NOPE))
    return np.concatenate(cols).astype(np.int32)


def _kvb_perm():
    per = MLA_NOPE + MLA_V
    k = [per * h + np.arange(MLA_NOPE) for h in range(MLA_HEADS)]
    v = [per * h + MLA_NOPE + np.arange(MLA_V) for h in range(MLA_HEADS)]
    return np.concatenate(k + v).astype(np.int32)


_IN_PERM, _QB_PERM, _KVB_PERM = _in_perm(), _qb_perm(), _kvb_perm()


def _take_cols(w, perm):
    ncol = w.shape[-1]
    same_run = lambda k: (perm[k] == ncol) if perm[k - 1] == ncol else (perm[k] == perm[k - 1] + 1 and perm[k] != ncol)
    cuts = [0] + [k for k in range(1, len(perm)) if not same_run(k)] + [len(perm)]
    parts = []
    for a, b in zip(cuts[:-1], cuts[1:]):
        if perm[a] == ncol:
            parts.append(jnp.zeros(w.shape[:-1] + (b - a,), BF16))
        else:
            parts.append(w[..., int(perm[a]):int(perm[a]) + (b - a)].astype(BF16))
    return jnp.concatenate(parts, axis=-1)


def _params(*sem):
    return pltpu.CompilerParams(dimension_semantics=sem, vmem_limit_bytes=VMEM_LIMIT)


def _resident(block_shape, index_map):
    return pl.BlockSpec(block_shape, index_map, pipeline_mode=pl.Buffered(1))


def _rms(x, gain):
    ms = jnp.mean(x * x, axis=-1, keepdims=True)
    return x * lax.rsqrt(ms + RMS_EPS) * gain


def _ffn_kernel(x_ref, g_ref, wg_ref, wu_ref, wd_ref, fg_ref, o_ref, h_sc, acc_sc, *, final_norm):
    f = pl.program_id(1)

    @pl.when(f == 0)
    def _():
        h_sc[...] = _rms(x_ref[...], g_ref[...]).astype(BF16)
        acc_sc[...] = jnp.zeros_like(acc_sc)

    h = h_sc[...]
    gate = jnp.dot(h, wg_ref[...], preferred_element_type=F32)
    up = jnp.dot(h, wu_ref[...], preferred_element_type=F32)
    act = (gate * (1.0 / (1.0 + jnp.exp(-gate))) * up).astype(BF16)
    acc_sc[...] += jnp.dot(act, wd_ref[...], preferred_element_type=F32)

    @pl.when(f == pl.num_programs(1) - 1)
    def _():
        y = x_ref[...] + 0.5 * acc_sc[...]
        if final_norm:
            y = _rms(y, fg_ref[...])
        o_ref[...] = y


def _ffn(x, gain, wg, wu, wd, layer, final_gain, *, final_norm, tm=512, tf=512):
    T, D = x.shape
    F = wg.shape[-1]
    assert T % tm == 0 and F % tf == 0
    return pl.pallas_call(
        functools.partial(_ffn_kernel, final_norm=final_norm),
        grid=(T // tm, F // tf),
        in_specs=[
            pl.BlockSpec((tm, D), lambda i, f: (i, 0)),
            pl.BlockSpec((None, 1, D), lambda i, f: (layer, 0, 0)),
            pl.BlockSpec((None, D, tf), lambda i, f: (layer, 0, f)),
            pl.BlockSpec((None, D, tf), lambda i, f: (layer, 0, f)),
            pl.BlockSpec((None, tf, D), lambda i, f: (layer, f, 0)),
            pl.BlockSpec((1, D), lambda i, f: (0, 0)),
        ],
        out_specs=pl.BlockSpec((tm, D), lambda i, f: (i, 0)),
        out_shape=jax.ShapeDtypeStruct((T, D), F32),
        scratch_shapes=[pltpu.VMEM((tm, D), BF16), pltpu.VMEM((tm, D), F32)],
        compiler_params=_params("parallel", "arbitrary"),
        name="ffn",
    )(x, gain, wg, wu, wd, final_gain)


def _rope(x, cos, sin_signed):
    return x * cos + pltpu.roll(x, 2 * HALF, axis=1) * sin_signed


def _split3(x):
    a1 = x.astype(BF16)
    r1 = x - a1.astype(F32)
    a2 = r1.astype(BF16)
    a3 = (r1 - a2.astype(F32)).astype(BF16)
    return a1, a2, a3


def _store_chunks(ref, value):
    for c in range(ref.shape[0]):
        ref[c] = value[:, LANES * c:LANES * (c + 1)]


def _mixin_kernel(x_ref, g_ref, win_ref, qn_ref, wqb_ref, kvn_ref, wkvb_ref, cos_ref, sin_ref, fb_ref,
                  qmn_ref, qmp_ref, kmn_ref, kmr_ref, vm_ref, qs_ref, ks_ref, vs_ref,
                  qf_ref, kf_ref, vf_ref, qaug_ref, kaug_ref, carry_sc, *, steps_per_seq):
    i = pl.program_id(0)
    tm = x_ref.shape[0]
    h = _rms(x_ref[...], g_ref[...]).astype(BF16)
    cos = cos_ref[...]
    sin = sin_ref[...]
    lane = lax.broadcasted_iota(jnp.int32, (1, LANES), 1)

    def proj(a, b):
        return jnp.dot(h, win_ref[:, a:b], preferred_element_type=F32)

    cq = _rms(proj(_O_CQ, _O_CKV), qn_ref[...]).astype(BF16)
    q = jnp.dot(cq, wqb_ref[...], preferred_element_type=F32)
    q_scale = (MLA_NOPE + MLA_ROPE) ** -0.5 * LOG2E
    n_nope = MLA_HEADS * MLA_NOPE
    _store_chunks(qmn_ref, (q[:, :n_nope] * q_scale).astype(BF16))
    for c in range(MLA_HEADS // 2):
        sl = slice(LANES * c, LANES * (c + 1))
        qmp_ref[c] = (_rope(q[:, n_nope + LANES * c:n_nope + LANES * (c + 1)], cos, sin) * q_scale).astype(BF16)

    ckv = _rms(proj(_O_CKV, _O_KR), kvn_ref[...]).astype(BF16)
    kv = jnp.dot(ckv, wkvb_ref[...], preferred_element_type=F32)
    _store_chunks(kmn_ref, kv[:, :n_nope].astype(BF16))
    _store_chunks(vm_ref, kv[:, n_nope:].astype(BF16))
    kmr_ref[...] = _rope(proj(_O_KR, _O_QS), cos, sin).astype(BF16)

    qs = proj(_O_QS, _O_KS)
    for c in range(SWA_HEADS // 2):
        sl = slice(LANES * c, LANES * (c + 1))
        qs_ref[:, sl] = (_rope(qs[:, sl], cos, sin) * SWA_DIM ** -0.5).astype(BF16)
    ks = proj(_O_KS, _O_VS)
    for g in range(SWA_KV_HEADS):
        sl = slice(LANES * g, LANES * (g + 1))
        ks_ref[:, sl] = _rope(ks[:, sl], cos, sin).astype(BF16)
    vs_ref[...] = proj(_O_VS, _O_QF).astype(BF16)

    _store_chunks(qf_ref, (proj(_O_QF, _O_KF) * (FOX_DIM ** -0.5 * LOG2E)).astype(BF16))
    _store_chunks(kf_ref, proj(_O_KF, _O_VF).astype(BF16))
    _store_chunks(vf_ref, proj(_O_VF, _O_FG).astype(BF16))

    fl = proj(_O_FG, IN_COLS_R) + fb_ref[...]
    logf = jnp.minimum(fl, 0.0) - jnp.log(1.0 + jnp.exp(-jnp.abs(fl)))

    @pl.when(i % steps_per_seq == 0)
    def _():
        carry_sc[...] = jnp.zeros_like(carry_sc)

    row = lax.broadcasted_iota(jnp.int32, (tm, tm), 0)
    col = lax.broadcasted_iota(jnp.int32, (tm, tm), 1)
    tri = (col <= row).astype(BF16)
    c = carry_sc[...]
    for term in _split3(logf):
        c = c + jnp.dot(tri, term, preferred_element_type=F32)
    carry_sc[...] = c[tm - 1:tm, :]

    c1, c2, c3 = (term.astype(F32) for term in _split3(c * LOG2E))
    g8 = lane // FOX_HEADS
    ones = jnp.where(g8 < GATE_COPIES, 1.0, 0.0)
    qaug_ref[...] = jnp.where(g8 < 3, ones, jnp.where(g8 == 3, c1, jnp.where(g8 == 4, c2, jnp.where(g8 == 5, c3, 0.0)))).astype(BF16)
    kaug_ref[...] = jnp.where(g8 == 0, -c1, jnp.where(g8 == 1, -c2, jnp.where(g8 == 2, -c3, ones))).astype(BF16)


def _mixin(x, gain, w_in, qn, wqb, kvn, wkvb, cos, sin, fb, layer, *, batch, tm=256):
    T, D = x.shape
    S = T // batch
    assert S % tm == 0
    sps = S // tm

    def row(width):
        return pl.BlockSpec((tm, width), lambda i: (i, 0))

    def wres(a):
        return _resident((None,) + a.shape[1:], lambda i: (layer, 0, 0))

    def chunked(chunks):
        return (jax.ShapeDtypeStruct((batch, chunks, S, LANES), BF16),
                pl.BlockSpec((None, chunks, tm, LANES), lambda i: (i // sps, 0, i % sps, 0)))

    def flat(width):
        return jax.ShapeDtypeStruct((T, width), BF16), row(width)

    outs = dict(qmn=chunked(MLA_HEADS), qmp=chunked(MLA_HEADS // 2), kmn=chunked(MLA_HEADS), kmr=flat(LANES),
                vm=chunked(MLA_HEADS), qs=flat(SWA_DIM * SWA_HEADS), ks=flat(LANES * SWA_KV_HEADS),
                vs=flat(LANES * SWA_KV_HEADS), qf=chunked(FOX_HEADS // 2), kf=chunked(FOX_HEADS // 2),
                vf=chunked(FOX_HEADS // 2), qaug=flat(LANES), kaug=flat(LANES))
    return pl.pallas_call(
        functools.partial(_mixin_kernel, steps_per_seq=sps),
        grid=(T // tm,),
        in_specs=[row(D), wres(gain), wres(w_in), wres(qn), wres(wqb), wres(kvn), wres(wkvb),
                  row(LANES), row(LANES), wres(fb)],
        out_specs=[spec for _, spec in outs.values()],
        out_shape=[shape for shape, _ in outs.values()],
        scratch_shapes=[pltpu.VMEM((1, LANES), F32)],
        compiler_params=_params("arbitrary"),
        name="mixin",
    )(x, gain, w_in, qn, wqb, kvn, wkvb, cos, sin, fb)


def _qk(q, k):
    return lax.dot_general(q, k, (((1,), (1,)), ((), ())), preferred_element_type=F32)


def _causal_mask(t):
    row = lax.broadcasted_iota(jnp.int32, (t, t), 0)
    col = lax.broadcasted_iota(jnp.int32, (t, t), 1)
    return col <= row


def _flash_kernel(tab_ref, qa_ref, qb_ref, ka_ref, kb_ref, v_ref, o_ref,
                  q_sc, k_sc, v_sc, m_sc, acc_sc, *bufs, kind, t, n_off):
    h = pl.program_id(1)
    n = q_sc.shape[0] // t
    lane = lax.broadcasted_iota(jnp.int32, (1, LANES), 1)
    even = (h % 2) == 0
    zero = jnp.zeros((), BF16)

    if kind == "mla":
        mine = ((lane % (2 * HALF)) < HALF) == even
        q_sc[:, :LANES] = qa_ref[...]
        q_sc[:, LANES:] = qb_ref[...]
        k_sc[:, :LANES] = ka_ref[...]
        k_sc[:, LANES:] = jnp.where(mine, kb_ref[...], zero)
        v_sc[:, :LANES] = v_ref[...]
    else:
        mine = (lane < FOX_DIM) == even
        gate = ((lane % FOX_HEADS) == h) & (lane < GATE_COPIES * FOX_HEADS)
        q_sc[:, :LANES] = qa_ref[...]
        q_sc[:, LANES:] = jnp.where(gate, qb_ref[...], zero)
        k_sc[:, :LANES] = jnp.where(mine, ka_ref[...], zero)
        k_sc[:, LANES:] = kb_ref[...]
        v_sc[:, :LANES] = jnp.where(mine, v_ref[...], zero)
    v_sc[:, LANES:] = jnp.ones((v_sc.shape[0], LANES), BF16)

    def rows(j):
        return pl.ds(pl.multiple_of(j * t, t), t)

    s_bufs, p_bufs, a_bufs = bufs[:FLASH_BUFS], bufs[FLASH_BUFS:2 * FLASH_BUFS], bufs[2 * FLASH_BUFS:]

    def scores(i, j, dst):
        dst[...] = _qk(q_sc[rows(i), :], k_sc[rows(j), :])

    def softmax(par, i, first):
        s = s_bufs[par][...]
        if first:
            s = jnp.where(_causal_mask(t), s, NEG)
            m_new = jnp.max(s, axis=-1, keepdims=True)
        else:
            m_old = m_sc[rows(i), :]
            m_new = jnp.maximum(m_old, jnp.max(s, axis=-1, keepdims=True))
            a_bufs[par][...] = jnp.exp2(m_old - m_new)
        p_bufs[par][...] = jnp.exp2(s - m_new).astype(BF16)
        m_sc[rows(i), :] = m_new

    def accumulate(par, i, j, first):
        pv = jnp.dot(p_bufs[par][...], v_sc[rows(j), :], preferred_element_type=F32)
        if first:
            acc_sc[rows(i), :] = pv
        else:
            acc_sc[rows(i), :] = a_bufs[par][...] * acc_sc[rows(i), :] + pv

    def pipeline(base, count, first, lag):
        nb = 2 * lag
        blk = lambda a: (tab_ref[0, base + a], tab_ref[1, base + a])

        def step(a, u, with_acc=True):
            if with_acc:
                accumulate((u - lag) % nb, *blk(a - lag), first)
            scores(*blk(a + lag), s_bufs[(u + lag) % nb])
            softmax(u, blk(a)[0], first)

        for a in range(lag):
            scores(*blk(a), s_bufs[a])
        for a in range(min(lag, count)):
            step(a, a, with_acc=False)
        main = max(count - lag, 0)

        def body(k, carry):
            for u in range(nb):
                step(lag + nb * k + u, (lag + u) % nb)
            return carry

        lax.fori_loop(0, main // nb, body, 0)
        for a in range(lag + main // nb * nb, count):
            step(a, a % nb)
        for a in range(main, count):
            accumulate(a % nb, *blk(a), first)

    pipeline(0, n, True, 1)
    pipeline(n, n_off, False, FLASH_LAG)

    def finish(i, carry):
        o_ref[rows(i), :] = (acc_sc[rows(i), :LANES] / acc_sc[rows(i), LANES:]).astype(o_ref.dtype)
        return carry

    lax.fori_loop(0, n, finish, 0)


def _flash_attention(qa, qb, ka, kb, v, *, kind, t=512):
    B, _, S, _ = qa.shape
    heads = MLA_HEADS if kind == "mla" else FOX_HEADS
    n = S // t
    assert S % t == 0 and n % 2 == 0
    diag = [(i, i) for i in range(n)]
    off = [(i, j) for j in range(n - 1) for i in range(j + 1, n)]
    table = jnp.asarray(np.array(diag + off + off[-1:] * (2 * FLASH_LAG), np.int32).T)
    shared = lambda **kw: pl.BlockSpec((None, S, LANES), lambda b, h: (b, 0, 0), **kw)
    own = lambda **kw: pl.BlockSpec((None, None, S, LANES), lambda b, h: (b, h, 0, 0), **kw)
    pair = lambda **kw: pl.BlockSpec((None, None, S, LANES), lambda b, h: (b, h // 2, 0, 0), **kw)
    single = dict(pipeline_mode=pl.Buffered(1))
    if kind == "mla":
        in_specs = [own(**single), pair(**single), own(), shared(), own()]
    else:
        in_specs = [pair(**single), shared(**single), pair(), shared(), pair()]
    return pl.pallas_call(
        functools.partial(_flash_kernel, kind=kind, t=t, n_off=len(off)),
        grid=(B, heads),
        in_specs=[pl.BlockSpec(memory_space=pltpu.SMEM)] + in_specs,
        out_specs=own(),
        out_shape=jax.ShapeDtypeStruct((B, heads, S, LANES), BF16),
        scratch_shapes=[pltpu.VMEM((S, 2 * LANES), BF16), pltpu.VMEM((S, 2 * LANES), BF16),
                        pltpu.VMEM((S, 2 * LANES), BF16),
                        pltpu.VMEM((S, 1), F32), pltpu.VMEM((S, 2 * LANES), F32)]
                       + [pltpu.VMEM((t, t), F32)] * FLASH_BUFS
                       + [pltpu.VMEM((t, t), BF16)] * FLASH_BUFS
                       + [pltpu.VMEM((t, 1), F32)] * FLASH_BUFS,
        compiler_params=pltpu.CompilerParams(dimension_semantics=("parallel", "arbitrary"),
                                             vmem_limit_bytes=FLASH_VMEM_LIMIT),
        name=kind + "_attn",
    )(table, qa, qb, ka, kb, v)


def _swa_kernel(sink_ref, q_ref, kc_ref, kp_ref, vc_ref, vp_ref, o_ref):
    i = pl.program_id(1)
    t = q_ref.shape[0]
    lane = lax.broadcasted_iota(jnp.int32, (1, LANES), 1)
    pair_lo = (lane % (2 * HALF)) < HALF
    k_all = jnp.concatenate([kp_ref[...], kc_ref[...]], axis=0)
    v_all = jnp.concatenate([vp_ref[...], vc_ref[...]], axis=0)
    row = lax.broadcasted_iota(jnp.int32, (t, WINDOW + t), 0)
    col = lax.broadcasted_iota(jnp.int32, (t, WINDOW + t), 1)
    valid = (col <= row + WINDOW) & (col > row) & ((col >= WINDOW) | (i > 0))
    group = SWA_HEADS // SWA_KV_HEADS
    zero = jnp.zeros((), BF16)
    for c in range(SWA_HEADS // 2):
        g = (2 * c) // group
        q = q_ref[:, LANES * c:LANES * (c + 1)]
        kg = k_all[:, LANES * g:LANES * (g + 1)]
        vg = v_all[:, LANES * g:LANES * (g + 1)]
        outs = []
        for e in range(2):
            ke = jnp.where(pair_lo, kg, zero) if e == 0 else jnp.where(pair_lo, zero, kg)
            sink = sink_ref[2 * c + e]
            s = jnp.where(valid, _qk(q, ke), NEG)
            m = jnp.maximum(jnp.max(s, axis=-1, keepdims=True), sink)
            p = jnp.exp(s - m)
            l = jnp.sum(p, axis=-1, keepdims=True) + jnp.exp(sink - m)
            outs.append(jnp.dot(p.astype(BF16), vg, preferred_element_type=F32) / l)
        o_ref[:, LANES * c:LANES * (c + 1)] = jnp.where(lane < SWA_DIM, outs[0], outs[1]).astype(o_ref.dtype)


def _swa_attention(sinks, q, k, v, *, t=256):
    B, S, _ = q.shape
    assert S % t == 0 and t % WINDOW == 0
    r = t // WINDOW
    kv_w = LANES * SWA_KV_HEADS
    cur = pl.BlockSpec((None, t, kv_w), lambda b, i: (b, i, 0))
    prev = pl.BlockSpec((None, WINDOW, kv_w), lambda b, i: (b, jnp.maximum(i * r - 1, 0), 0))
    return pl.pallas_call(
        _swa_kernel,
        grid=(B, S // t),
        in_specs=[
            pl.BlockSpec(memory_space=pltpu.SMEM),
            pl.BlockSpec((None, t, SWA_HEADS * SWA_DIM), lambda b, i: (b, i, 0)),
            cur, prev, cur, prev,
        ],
        out_specs=pl.BlockSpec((None, t, SWA_HEADS * SWA_DIM), lambda b, i: (b, i, 0)),
        out_shape=jax.ShapeDtypeStruct((B, S, SWA_HEADS * SWA_DIM), BF16),
        compiler_params=_params("parallel", "arbitrary"),
        name="swa_attn",
    )(sinks, q, k, k, v, v)


def _outproj_kernel(x_ref, om_ref, os_ref, of_ref, w_ref, o_ref):
    a = om_ref.shape[0] * LANES
    b = a + os_ref.shape[1]
    om = jnp.concatenate([om_ref[h] for h in range(om_ref.shape[0])], axis=1)
    y = jnp.dot(om, w_ref[:a, :], preferred_element_type=F32)
    y += jnp.dot(os_ref[...], w_ref[a:b, :], preferred_element_type=F32)
    of = jnp.concatenate([of_ref[2 * c] + of_ref[2 * c + 1] for c in range(of_ref.shape[0] // 2)], axis=1)
    y += jnp.dot(of, w_ref[b:, :], preferred_element_type=F32)
    o_ref[...] = x_ref[...] + y


def _outproj(x, om, osw, of, w_out, layer, *, tm=512):
    T, D = x.shape
    S = om.shape[2]
    assert S % tm == 0
    sps = S // tm

    def row(width):
        return pl.BlockSpec((tm, width), lambda i: (i, 0))

    def heads(a):
        return pl.BlockSpec((None, a.shape[1], tm, LANES), lambda i: (i // sps, 0, i % sps, 0))

    return pl.pallas_call(
        _outproj_kernel,
        grid=(T // tm,),
        in_specs=[row(D), heads(om), row(osw.shape[1]), heads(of),
                  _resident((None,) + w_out.shape[1:], lambda i: (layer, 0, 0))],
        out_specs=row(D),
        out_shape=jax.ShapeDtypeStruct((T, D), F32),
        compiler_params=_params("parallel"),
        name="outproj",
    )(x, om, osw, of, w_out)


def kernel(x, positions, ffn1_norm, ffn1_w_gate, ffn1_w_up, ffn1_w_down, mix_norm, w_in, mla_q_norm, mla_w_q_b, mla_kv_norm, mla_w_kv_b, swa_sinks, fox_forget_bias, w_out, ffn2_norm, ffn2_w_gate, ffn2_w_up, ffn2_w_down, final_norm):
    B, S, D = x.shape
    depth = w_in.shape[0]
    T = B * S

    bf = lambda w: w.astype(BF16)
    w_in_r = _take_cols(w_in, _IN_PERM)
    wqb_r = _take_cols(mla_w_q_b, _QB_PERM)
    wkvb_r = _take_cols(mla_w_kv_b, _KVB_PERM)
    ffn_w = [(bf(ffn1_w_gate), bf(ffn1_w_up), bf(ffn1_w_down)), (bf(ffn2_w_gate), bf(ffn2_w_up), bf(ffn2_w_down))]
    ffn_g = [ffn1_norm[:, None, :], ffn2_norm[:, None, :]]
    w_out_b = bf(w_out)
    mix_g = mix_norm[:, None, :]
    qn = mla_q_norm[:, None, :]
    kvn = mla_kv_norm[:, None, :]
    fb = jnp.pad(jnp.tile(fox_forget_bias, (1, GATE_COPIES)), ((0, 0), (0, LANES - GATE_COPIES * FOX_HEADS)))[:, None, :]
    fin_g = final_norm[None, :]

    inv_freq = ROPE_THETA ** (-jnp.arange(0, 2 * HALF, 2, dtype=F32) / (2 * HALF))
    ang = positions.astype(F32).reshape(T, 1) * inv_freq
    cos, sin = jnp.cos(ang), jnp.sin(ang)
    cos_t = jnp.concatenate([cos, cos, cos, cos], axis=-1)
    sin_t = jnp.concatenate([-sin, -sin, sin, sin], axis=-1)

    xt = x.reshape(T, D)
    sh = lambda a: a.reshape(B, S, a.shape[-1])
    for l in range(depth):
        xt = _ffn(xt, ffn_g[0], *ffn_w[0], l, fin_g, final_norm=False)
        qmn, qmp, kmn, kmr, vm, qs, ks, vs, qf, kf, vf, qaug, kaug = _mixin(
            xt, mix_g, w_in_r, qn, wqb_r, kvn, wkvb_r, cos_t, sin_t, fb, l, batch=B)
        o_mla = _flash_attention(qmn, qmp, kmn, sh(kmr), vm, kind="mla")
        o_swa = _swa_attention(swa_sinks[l], sh(qs), sh(ks), sh(vs))
        o_fox = _flash_attention(qf, sh(qaug), kf, sh(kaug), vf, kind="fox")
        xt = _outproj(xt, o_mla, o_swa.reshape(T, -1), o_fox, w_out_b, l)
        xt = _ffn(xt, ffn_g[1], *ffn_w[1], l, fin_g, final_norm=(l == depth - 1))
    return xt.reshape(B, S, D)
```

```python
import functools
import math

import numpy as np
import jax
import jax.numpy as jnp
from jax import lax
from jax.experimental import pallas as pl
from jax.experimental.pallas import tpu as pltpu

RMS_EPS = 1e-6
ROPE_THETA = 10000.0

MLA_HEADS = 8
MLA_Q_LORA = 512
MLA_KV_LORA = 256
MLA_NOPE = 128
MLA_ROPE = 64
MLA_V = 128

SWA_HEADS = 8
SWA_KV_HEADS = 2
SWA_DIM = 64
WINDOW = 128

FOX_HEADS = 8
FOX_DIM = 64

LANES = 128
HALF = 32
VMEM_LIMIT = 56 * 1024 * 1024
FLASH_VMEM_LIMIT = 60 * 1024 * 1024
FLASH_LAG = 2
FLASH_BUFS = 2 * FLASH_LAG
FLASH_UNROLL = 7
NEG = -1e30
LOG2E = math.log2(math.e)
GATE_COPIES = 6

F32 = jnp.float32
BF16 = jnp.bfloat16

_O_CQ, _O_CKV, _O_KR, _O_QS, _O_KS, _O_VS, _O_QF, _O_KF, _O_VF, _O_FG, IN_COLS_R = (
    0, 512, 768, 896, 1408, 1664, 1920, 2432, 2944, 3456, 3584)


def _chunk_pair(base_a, base_b):
    r = np.arange(HALF)
    return np.concatenate([base_a + r, base_b + r, base_a + HALF + r, base_b + HALF + r])


def _in_perm():
    src = [0, 512, 768, 832, 1344, 1472, 1600, 2112, 2624, 3136]
    zero = 3144
    cols = [np.arange(src[0], src[0] + 512), np.arange(src[1], src[1] + 256)]
    cols.append(_chunk_pair(src[2], src[2]))
    for c in range(SWA_HEADS // 2):
        cols.append(_chunk_pair(src[3] + SWA_DIM * 2 * c, src[3] + SWA_DIM * (2 * c + 1)))
    for g in range(SWA_KV_HEADS):
        cols.append(_chunk_pair(src[4] + SWA_DIM * g, src[4] + SWA_DIM * g))
    for g in range(SWA_KV_HEADS):
        v = src[5] + SWA_DIM * g + np.arange(SWA_DIM)
        cols.append(np.concatenate([v, v]))
    cols += [np.arange(src[6], src[6] + 512), np.arange(src[7], src[7] + 512), np.arange(src[8], src[8] + 512)]
    gate = np.tile(np.arange(src[9], src[9] + FOX_HEADS), GATE_COPIES)
    cols.append(np.concatenate([gate, np.full(LANES - gate.shape[0], zero)]))
    out = np.concatenate(cols).astype(np.int32)
    assert out.shape[0] == IN_COLS_R
    return out


def _qb_perm():
    per = MLA_NOPE + MLA_ROPE
    cols = [per * h + np.arange(MLA_NOPE) for h in range(MLA_HEADS)]
    for c in range(MLA_HEADS // 2):
        cols.append(_chunk_pair(per * 2 * c + MLA_NOPE, per * (2 * c + 1) + MLA_NOPE))
    return np.concatenate(cols).astype(np.int32)


def _kvb_perm():
    per = MLA_NOPE + MLA_V
    k = [per * h + np.arange(MLA_NOPE) for h in range(MLA_HEADS)]
    v = [per * h + MLA_NOPE + np.arange(MLA_V) for h in range(MLA_HEADS)]
    return np.concatenate(k + v).astype(np.int32)


_IN_PERM, _QB_PERM, _KVB_PERM = _in_perm(), _qb_perm(), _kvb_perm()


def _take_cols(w, perm):
    ncol = w.shape[-1]
    same_run = lambda k: (perm[k] == ncol) if perm[k - 1] == ncol else (perm[k] == perm[k - 1] + 1 and perm[k] != ncol)
    cuts = [0] + [k for k in range(1, len(perm)) if not same_run(k)] + [len(perm)]
    parts = []
    for a, b in zip(cuts[:-1], cuts[1:]):
        if perm[a] == ncol:
            parts.append(jnp.zeros(w.shape[:-1] + (b - a,), BF16))
        else:
            parts.append(w[..., int(perm[a]):int(perm[a]) + (b - a)].astype(BF16))
    return jnp.concatenate(parts, axis=-1)


def _params(*sem):
    return pltpu.CompilerParams(dimension_semantics=sem, vmem_limit_bytes=VMEM_LIMIT)


def _resident(block_shape, index_map):
    return pl.BlockSpec(block_shape, index_map, pipeline_mode=pl.Buffered(1))


def _rms(x, gain):
    ms = jnp.mean(x * x, axis=-1, keepdims=True)
    return x * lax.rsqrt(ms + RMS_EPS) * gain


def _ffn_kernel(x_ref, g_ref, wg_ref, wu_ref, wd_ref, fg_ref, o_ref, h_sc, acc_sc, *, final_norm):
    f = pl.program_id(1)

    @pl.when(f == 0)
    def _():
        h_sc[...] = _rms(x_ref[...], g_ref[...]).astype(BF16)
        acc_sc[...] = jnp.zeros_like(acc_sc)

    h = h_sc[...]
    gate = jnp.dot(h, wg_ref[...], preferred_element_type=F32)
    up = jnp.dot(h, wu_ref[...], preferred_element_type=F32)
    act = (gate * (1.0 / (1.0 + jnp.exp(-gate))) * up).astype(BF16)
    acc_sc[...] += jnp.dot(act, wd_ref[...], preferred_element_type=F32)

    @pl.when(f == pl.num_programs(1) - 1)
    def _():
        y = x_ref[...] + 0.5 * acc_sc[...]
        if final_norm:
            y = _rms(y, fg_ref[...])
        o_ref[...] = y


def _ffn(x, gain, wg, wu, wd, layer, final_gain, *, final_norm, tm=512, tf=512):
    T, D = x.shape
    F = wg.shape[-1]
    assert T % tm == 0 and F % tf == 0
    return pl.pallas_call(
        functools.partial(_ffn_kernel, final_norm=final_norm),
        grid=(T // tm, F // tf),
        in_specs=[
            pl.BlockSpec((tm, D), lambda i, f: (i, 0)),
            pl.BlockSpec((None, 1, D), lambda i, f: (layer, 0, 0)),
            pl.BlockSpec((None, D, tf), lambda i, f: (layer, 0, f)),
            pl.BlockSpec((None, D, tf), lambda i, f: (layer, 0, f)),
            pl.BlockSpec((None, tf, D), lambda i, f: (layer, f, 0)),
            pl.BlockSpec((1, D), lambda i, f: (0, 0)),
        ],
        out_specs=pl.BlockSpec((tm, D), lambda i, f: (i, 0)),
        out_shape=jax.ShapeDtypeStruct((T, D), F32),
        scratch_shapes=[pltpu.VMEM((tm, D), BF16), pltpu.VMEM((tm, D), F32)],
        compiler_params=_params("parallel", "arbitrary"),
        name="ffn",
    )(x, gain, wg, wu, wd, final_gain)


def _rope(x, cos, sin_signed):
    return x * cos + pltpu.roll(x, 2 * HALF, axis=1) * sin_signed


def _split3(x):
    a1 = x.astype(BF16)
    r1 = x - a1.astype(F32)
    a2 = r1.astype(BF16)
    a3 = (r1 - a2.astype(F32)).astype(BF16)
    return a1, a2, a3


def _store_chunks(ref, value):
    for c in range(ref.shape[0]):
        ref[c] = value[:, LANES * c:LANES * (c + 1)]


def _mixin_kernel(x_ref, g_ref, win_ref, qn_ref, wqb_ref, kvn_ref, wkvb_ref, cos_ref, sin_ref, fb_ref,
                  qmn_ref, qmp_ref, kmn_ref, kmr_ref, vm_ref, qs_ref, ks_ref, vs_ref,
                  qf_ref, kf_ref, vf_ref, qaug_ref, kaug_ref, carry_sc, *, steps_per_seq):
    i = pl.program_id(0)
    tm = x_ref.shape[0]
    h = _rms(x_ref[...], g_ref[...]).astype(BF16)
    cos = cos_ref[...]
    sin = sin_ref[...]
    lane = lax.broadcasted_iota(jnp.int32, (1, LANES), 1)

    def proj(a, b):
        return jnp.dot(h, win_ref[:, a:b], preferred_element_type=F32)

    cq = _rms(proj(_O_CQ, _O_CKV), qn_ref[...]).astype(BF16)
    q = jnp.dot(cq, wqb_ref[...], preferred_element_type=F32)
    q_scale = (MLA_NOPE + MLA_ROPE) ** -0.5 * LOG2E
    n_nope = MLA_HEADS * MLA_NOPE
    _store_chunks(qmn_ref, (q[:, :n_nope] * q_scale).astype(BF16))
    for c in range(MLA_HEADS // 2):
        sl = slice(LANES * c, LANES * (c + 1))
        qmp_ref[c] = (_rope(q[:, n_nope + LANES * c:n_nope + LANES * (c + 1)], cos, sin) * q_scale).astype(BF16)

    ckv = _rms(proj(_O_CKV, _O_KR), kvn_ref[...]).astype(BF16)
    kv = jnp.dot(ckv, wkvb_ref[...], preferred_element_type=F32)
    _store_chunks(kmn_ref, kv[:, :n_nope].astype(BF16))
    _store_chunks(vm_ref, kv[:, n_nope:].astype(BF16))
    kmr_ref[...] = _rope(proj(_O_KR, _O_QS), cos, sin).astype(BF16)

    qs = proj(_O_QS, _O_KS)
    for c in range(SWA_HEADS // 2):
        sl = slice(LANES * c, LANES * (c + 1))
        qs_ref[:, sl] = (_rope(qs[:, sl], cos, sin) * SWA_DIM ** -0.5).astype(BF16)
    ks = proj(_O_KS, _O_VS)
    for g in range(SWA_KV_HEADS):
        sl = slice(LANES * g, LANES * (g + 1))
        ks_ref[:, sl] = _rope(ks[:, sl], cos, sin).astype(BF16)
    vs_ref[...] = proj(_O_VS, _O_QF).astype(BF16)

    _store_chunks(qf_ref, (proj(_O_QF, _O_KF) * (FOX_DIM ** -0.5 * LOG2E)).astype(BF16))
    _store_chunks(kf_ref, proj(_O_KF, _O_VF).astype(BF16))
    _store_chunks(vf_ref, proj(_O_VF, _O_FG).astype(BF16))

    fl = proj(_O_FG, IN_COLS_R) + fb_ref[...]
    logf = jnp.minimum(fl, 0.0) - jnp.log(1.0 + jnp.exp(-jnp.abs(fl)))

    @pl.when(i % steps_per_seq == 0)
    def _():
        carry_sc[...] = jnp.zeros_like(carry_sc)

    row = lax.broadcasted_iota(jnp.int32, (tm, tm), 0)
    col = lax.broadcasted_iota(jnp.int32, (tm, tm), 1)
    tri = (col <= row).astype(BF16)
    c = carry_sc[...]
    for term in _split3(logf):
        c = c + jnp.dot(tri, term, preferred_element_type=F32)
    carry_sc[...] = c[tm - 1:tm, :]

    c1, c2, c3 = (term.astype(F32) for term in _split3(c * LOG2E))
    g8 = lane // FOX_HEADS
    ones = jnp.where(g8 < GATE_COPIES, 1.0, 0.0)
    qaug_ref[...] = jnp.where(g8 < 3, ones, jnp.where(g8 == 3, c1, jnp.where(g8 == 4, c2, jnp.where(g8 == 5, c3, 0.0)))).astype(BF16)
    kaug_ref[...] = jnp.where(g8 == 0, -c1, jnp.where(g8 == 1, -c2, jnp.where(g8 == 2, -c3, ones))).astype(BF16)


def _mixin(x, gain, w_in, qn, wqb, kvn, wkvb, cos, sin, fb, layer, *, batch, tm=256):
    T, D = x.shape
    S = T // batch
    assert S % tm == 0
    sps = S // tm

    def row(width):
        return pl.BlockSpec((tm, width), lambda i: (i, 0))

    def wres(a):
        return _resident((None,) + a.shape[1:], lambda i: (layer, 0, 0))

    def chunked(chunks):
        return (jax.ShapeDtypeStruct((batch, chunks, S, LANES), BF16),
                pl.BlockSpec((None, chunks, tm, LANES), lambda i: (i // sps, 0, i % sps, 0)))

    def flat(width):
        return jax.ShapeDtypeStruct((T, width), BF16), row(width)

    outs = dict(qmn=chunked(MLA_HEADS), qmp=chunked(MLA_HEADS // 2), kmn=chunked(MLA_HEADS), kmr=flat(LANES),
                vm=chunked(MLA_HEADS), qs=flat(SWA_DIM * SWA_HEADS), ks=flat(LANES * SWA_KV_HEADS),
                vs=flat(LANES * SWA_KV_HEADS), qf=chunked(FOX_HEADS // 2), kf=chunked(FOX_HEADS // 2),
                vf=chunked(FOX_HEADS // 2), qaug=flat(LANES), kaug=flat(LANES))
    return pl.pallas_call(
        functools.partial(_mixin_kernel, steps_per_seq=sps),
        grid=(T // tm,),
        in_specs=[row(D), wres(gain), wres(w_in), wres(qn), wres(wqb), wres(kvn), wres(wkvb),
                  row(LANES), row(LANES), wres(fb)],
        out_specs=[spec for _, spec in outs.values()],
        out_shape=[shape for shape, _ in outs.values()],
        scratch_shapes=[pltpu.VMEM((1, LANES), F32)],
        compiler_params=_params("arbitrary"),
        name="mixin",
    )(x, gain, w_in, qn, wqb, kvn, wkvb, cos, sin, fb)


def _qk(q, k):
    return lax.dot_general(q, k, (((1,), (1,)), ((), ())), preferred_element_type=F32)


def _causal_mask(t):
    row = lax.broadcasted_iota(jnp.int32, (t, t), 0)
    col = lax.broadcasted_iota(jnp.int32, (t, t), 1)
    return col <= row


def _flash_kernel(tab_ref, qa_ref, qb_ref, ka_ref, kb_ref, v_ref, o_ref,
                  q_sc, k_sc, v_sc, m_sc, acc_sc, *bufs, kind, t, n_off):
    h = pl.program_id(1)
    n = q_sc.shape[0] // t
    lane = lax.broadcasted_iota(jnp.int32, (1, LANES), 1)
    even = (h % 2) == 0
    zero = jnp.zeros((), BF16)

    if kind == "mla":
        mine = ((lane % (2 * HALF)) < HALF) == even
        q_sc[:, :LANES] = qa_ref[...]
        q_sc[:, LANES:] = qb_ref[...]
        k_sc[:, :LANES] = ka_ref[...]
        k_sc[:, LANES:] = jnp.where(mine, kb_ref[...], zero)
        v_sc[:, :LANES] = v_ref[...]
    else:
        mine = (lane < FOX_DIM) == even
        gate = ((lane % FOX_HEADS) == h) & (lane < GATE_COPIES * FOX_HEADS)
        q_sc[:, :LANES] = qa_ref[...]
        q_sc[:, LANES:] = jnp.where(gate, qb_ref[...], zero)
        k_sc[:, :LANES] = jnp.where(mine, ka_ref[...], zero)
        k_sc[:, LANES:] = kb_ref[...]
        v_sc[:, :LANES] = jnp.where(mine, v_ref[...], zero)
    v_sc[:, LANES:] = jnp.ones((v_sc.shape[0], LANES), BF16)

    def rows(j):
        return pl.ds(pl.multiple_of(j * t, t), t)

    s_bufs, p_bufs, a_bufs, r_bufs = (bufs[FLASH_BUFS * g:FLASH_BUFS * (g + 1)] for g in range(4))

    def scores(i, j, slot):
        s = _qk(q_sc[rows(i), :], k_sc[rows(j), :])
        s_bufs[slot][...] = s
        r_bufs[slot][...] = functools.reduce(jnp.maximum, [s[:, LANES * c:LANES * (c + 1)] for c in range(t // LANES)])

    def softmax(par, i, first):
        s = s_bufs[par][...]
        if first:
            s = jnp.where(_causal_mask(t), s, NEG)
        tiles = [s[:, LANES * c:LANES * (c + 1)] for c in range(t // LANES)]
        tile_max = functools.reduce(jnp.maximum, tiles) if first else r_bufs[par][...]
        m_new = jnp.broadcast_to(jnp.max(tile_max, axis=-1, keepdims=True), (t, LANES))
        if not first:
            m_old = m_sc[rows(i), :]
            m_new = jnp.maximum(m_old, m_new)
            a_bufs[par][...] = jnp.exp2(m_old - m_new)
        for c, tile in enumerate(tiles):
            p_bufs[par][:, LANES * c:LANES * (c + 1)] = jnp.exp2(tile - m_new).astype(BF16)
        m_sc[rows(i), :] = m_new

    def accumulate(par, i, j, first):
        pv = jnp.dot(p_bufs[par][...], v_sc[rows(j), :], preferred_element_type=F32)
        if first:
            acc_sc[rows(i), :] = pv
        else:
            a = a_bufs[par][...]
            acc_sc[rows(i), :] = jnp.concatenate([a, a], axis=1) * acc_sc[rows(i), :] + pv

    def pipeline(base, count, first, lag, unroll=1):
        nb = 2 * lag
        blk = lambda a: (tab_ref[0, base + a], tab_ref[1, base + a])

        def step(a, u, with_acc=True):
            if with_acc:
                accumulate((u - lag) % nb, *blk(a - lag), first)
            scores(*blk(a + lag), (u + lag) % nb)
            softmax(u, blk(a)[0], first)

        for a in range(lag):
            scores(*blk(a), a)
        for a in range(min(lag, count)):
            step(a, a, with_acc=False)
        main = max(count - lag, 0)

        span = nb * unroll

        def body(k, carry):
            for u in range(span):
                step(lag + span * k + u, (lag + u) % nb)
            return carry

        lax.fori_loop(0, main // span, body, 0)
        for a in range(lag + main // span * span, count):
            step(a, a % nb)
        for a in range(main, count):
            accumulate(a % nb, *blk(a), first)

    pipeline(0, n, True, 1)
    pipeline(n, n_off, False, FLASH_LAG, unroll=FLASH_UNROLL)

    def finish(i, carry):
        o_ref[rows(i), :] = (acc_sc[rows(i), :LANES] / acc_sc[rows(i), LANES:]).astype(o_ref.dtype)
        return carry

    lax.fori_loop(0, n, finish, 0)


def _flash_attention(qa, qb, ka, kb, v, *, kind, t=512):
    B, _, S, _ = qa.shape
    heads = MLA_HEADS if kind == "mla" else FOX_HEADS
    n = S // t
    assert S % t == 0 and n % 2 == 0
    diag = [(i, i) for i in range(n)]
    off = [(i, j) for j in range(n - 1) for i in range(j + 1, n)]
    table = jnp.asarray(np.array(diag + off + off[-1:] * (2 * FLASH_LAG), np.int32).T)
    shared = lambda **kw: pl.BlockSpec((None, S, LANES), lambda b, h: (b, 0, 0), **kw)
    own = lambda **kw: pl.BlockSpec((None, None, S, LANES), lambda b, h: (b, h, 0, 0), **kw)
    pair = lambda **kw: pl.BlockSpec((None, None, S, LANES), lambda b, h: (b, h // 2, 0, 0), **kw)
    single = dict(pipeline_mode=pl.Buffered(1))
    if kind == "mla":
        in_specs = [own(**single), pair(**single), own(), shared(), own()]
    else:
        in_specs = [pair(**single), shared(**single), pair(), shared(), pair()]
    return pl.pallas_call(
        functools.partial(_flash_kernel, kind=kind, t=t, n_off=len(off)),
        grid=(B, heads),
        in_specs=[pl.BlockSpec(memory_space=pltpu.SMEM)] + in_specs,
        out_specs=own(),
        out_shape=jax.ShapeDtypeStruct((B, heads, S, LANES), BF16),
        scratch_shapes=[pltpu.VMEM((S, 2 * LANES), BF16), pltpu.VMEM((S, 2 * LANES), BF16),
                        pltpu.VMEM((S, 2 * LANES), BF16),
                        pltpu.VMEM((S, LANES), F32), pltpu.VMEM((S, 2 * LANES), F32)]
                       + [pltpu.VMEM((t, t), F32)] * FLASH_BUFS
                       + [pltpu.VMEM((t, t), BF16)] * FLASH_BUFS
                       + [pltpu.VMEM((t, LANES), F32)] * FLASH_BUFS
                       + [pltpu.VMEM((t, LANES), F32)] * FLASH_BUFS,
        compiler_params=pltpu.CompilerParams(dimension_semantics=("parallel", "arbitrary"),
                                             vmem_limit_bytes=FLASH_VMEM_LIMIT),
        name=kind + "_attn",
    )(table, qa, qb, ka, kb, v)


def _swa_kernel(sink_ref, q_ref, kc_ref, kp_ref, vc_ref, vp_ref, o_ref):
    i = pl.program_id(1)
    t = q_ref.shape[0]
    lane = lax.broadcasted_iota(jnp.int32, (1, LANES), 1)
    pair_lo = (lane % (2 * HALF)) < HALF
    k_all = jnp.concatenate([kp_ref[...], kc_ref[...]], axis=0)
    v_all = jnp.concatenate([vp_ref[...], vc_ref[...]], axis=0)
    row = lax.broadcasted_iota(jnp.int32, (t, WINDOW + t), 0)
    col = lax.broadcasted_iota(jnp.int32, (t, WINDOW + t), 1)
    valid = (col <= row + WINDOW) & (col > row) & ((col >= WINDOW) | (i > 0))
    group = SWA_HEADS // SWA_KV_HEADS
    zero = jnp.zeros((), BF16)
    for c in range(SWA_HEADS // 2):
        g = (2 * c) // group
        q = q_ref[:, LANES * c:LANES * (c + 1)]
        kg = k_all[:, LANES * g:LANES * (g + 1)]
        vg = v_all[:, LANES * g:LANES * (g + 1)]
        outs = []
        for e in range(2):
            ke = jnp.where(pair_lo, kg, zero) if e == 0 else jnp.where(pair_lo, zero, kg)
            sink = sink_ref[2 * c + e]
            s = jnp.where(valid, _qk(q, ke), NEG)
            m = jnp.maximum(jnp.max(s, axis=-1, keepdims=True), sink)
            p = jnp.exp(s - m)
            l = jnp.sum(p, axis=-1, keepdims=True) + jnp.exp(sink - m)
            outs.append(jnp.dot(p.astype(BF16), vg, preferred_element_type=F32) / l)
        o_ref[:, LANES * c:LANES * (c + 1)] = jnp.where(lane < SWA_DIM, outs[0], outs[1]).astype(o_ref.dtype)


def _swa_attention(sinks, q, k, v, *, t=256):
    B, S, _ = q.shape
    assert S % t == 0 and t % WINDOW == 0
    r = t // WINDOW
    kv_w = LANES * SWA_KV_HEADS
    cur = pl.BlockSpec((None, t, kv_w), lambda b, i: (b, i, 0))
    prev = pl.BlockSpec((None, WINDOW, kv_w), lambda b, i: (b, jnp.maximum(i * r - 1, 0), 0))
    return pl.pallas_call(
        _swa_kernel,
        grid=(B, S // t),
        in_specs=[
            pl.BlockSpec(memory_space=pltpu.SMEM),
            pl.BlockSpec((None, t, SWA_HEADS * SWA_DIM), lambda b, i: (b, i, 0)),
            cur, prev, cur, prev,
        ],
        out_specs=pl.BlockSpec((None, t, SWA_HEADS * SWA_DIM), lambda b, i: (b, i, 0)),
        out_shape=jax.ShapeDtypeStruct((B, S, SWA_HEADS * SWA_DIM), BF16),
        compiler_params=_params("parallel", "arbitrary"),
        name="swa_attn",
    )(sinks, q, k, k, v, v)


def _outproj_kernel(x_ref, om_ref, os_ref, of_ref, w_ref, o_ref):
    a = om_ref.shape[0] * LANES
    b = a + os_ref.shape[1]
    om = jnp.concatenate([om_ref[h] for h in range(om_ref.shape[0])], axis=1)
    y = jnp.dot(om, w_ref[:a, :], preferred_element_type=F32)
    y += jnp.dot(os_ref[...], w_ref[a:b, :], preferred_element_type=F32)
    of = jnp.concatenate([of_ref[2 * c] + of_ref[2 * c + 1] for c in range(of_ref.shape[0] // 2)], axis=1)
    y += jnp.dot(of, w_ref[b:, :], preferred_element_type=F32)
    o_ref[...] = x_ref[...] + y


def _outproj(x, om, osw, of, w_out, layer, *, tm=512):
    T, D = x.shape
    S = om.shape[2]
    assert S % tm == 0
    sps = S // tm

    def row(width):
        return pl.BlockSpec((tm, width), lambda i: (i, 0))

    def heads(a):
        return pl.BlockSpec((None, a.shape[1], tm, LANES), lambda i: (i // sps, 0, i % sps, 0))

    return pl.pallas_call(
        _outproj_kernel,
        grid=(T // tm,),
        in_specs=[row(D), heads(om), row(osw.shape[1]), heads(of),
                  _resident((None,) + w_out.shape[1:], lambda i: (layer, 0, 0))],
        out_specs=row(D),
        out_shape=jax.ShapeDtypeStruct((T, D), F32),
        compiler_params=_params("parallel"),
        name="outproj",
    )(x, om, osw, of, w_out)


def kernel(x, positions, ffn1_norm, ffn1_w_gate, ffn1_w_up, ffn1_w_down, mix_norm, w_in, mla_q_norm, mla_w_q_b, mla_kv_norm, mla_w_kv_b, swa_sinks, fox_forget_bias, w_out, ffn2_norm, ffn2_w_gate, ffn2_w_up, ffn2_w_down, final_norm):
    B, S, D = x.shape
    depth = w_in.shape[0]
    T = B * S

    bf = lambda w: w.astype(BF16)
    w_in_r = _take_cols(w_in, _IN_PERM)
    wqb_r = _take_cols(mla_w_q_b, _QB_PERM)
    wkvb_r = _take_cols(mla_w_kv_b, _KVB_PERM)
    ffn_w = [(bf(ffn1_w_gate), bf(ffn1_w_up), bf(ffn1_w_down)), (bf(ffn2_w_gate), bf(ffn2_w_up), bf(ffn2_w_down))]
    ffn_g = [ffn1_norm[:, None, :], ffn2_norm[:, None, :]]
    w_out_b = bf(w_out)
    mix_g = mix_norm[:, None, :]
    qn = mla_q_norm[:, None, :]
    kvn = mla_kv_norm[:, None, :]
    fb = jnp.pad(jnp.tile(fox_forget_bias, (1, GATE_COPIES)), ((0, 0), (0, LANES - GATE_COPIES * FOX_HEADS)))[:, None, :]
    fin_g = final_norm[None, :]

    inv_freq = ROPE_THETA ** (-jnp.arange(0, 2 * HALF, 2, dtype=F32) / (2 * HALF))
    ang = positions.astype(F32).reshape(T, 1) * inv_freq
    cos, sin = jnp.cos(ang), jnp.sin(ang)
    cos_t = jnp.concatenate([cos, cos, cos, cos], axis=-1)
    sin_t = jnp.concatenate([-sin, -sin, sin, sin], axis=-1)

    xt = x.reshape(T, D)
    sh = lambda a: a.reshape(B, S, a.shape[-1])
    for l in range(depth):
        xt = _ffn(xt, ffn_g[0], *ffn_w[0], l, fin_g, final_norm=False)
        qmn, qmp, kmn, kmr, vm, qs, ks, vs, qf, kf, vf, qaug, kaug = _mixin(
            xt, mix_g, w_in_r, qn, wqb_r, kvn, wkvb_r, cos_t, sin_t, fb, l, batch=B)
        o_mla = _flash_attention(qmn, qmp, kmn, sh(kmr), vm, kind="mla")
        o_swa = _swa_attention(swa_sinks[l], sh(qs), sh(ks), sh(vs))
        o_fox = _flash_attention(qf, sh(qaug), kf, sh(kaug), vf, kind="fox")
        xt = _outproj(xt, o_mla, o_swa.reshape(T, -1), o_fox, w_out_b, l)
        xt = _ffn(xt, ffn_g[1], *ffn_w[1], l, fin_g, final_norm=(l == depth - 1))
    return xt.reshape(B, S, D)
```

```python
import functools
import math

import numpy as np
import jax
import jax.numpy as jnp
from jax import lax
from jax.experimental import pallas as pl
from jax.experimental.pallas import tpu as pltpu

RMS_EPS = 1e-6
ROPE_THETA = 10000.0

MLA_HEADS = 8
MLA_Q_LORA = 512
MLA_KV_LORA = 256
MLA_NOPE = 128
MLA_ROPE = 64
MLA_V = 128

SWA_HEADS = 8
SWA_KV_HEADS = 2
SWA_DIM = 64
WINDOW = 128

FOX_HEADS = 8
FOX_DIM = 64

LANES = 128
HALF = 32
VMEM_LIMIT = 60 * 1024 * 1024
FFN_ROW_CHUNK = 256
FLASH_LAG = 2
FLASH_BUFS = 2 * FLASH_LAG
FLASH_UNROLL = 7
NEG = -1e30
LOG2E = math.log2(math.e)
GATE_COPIES = 6

F32 = jnp.float32
BF16 = jnp.bfloat16

_O_CQ, _O_CKV, _O_KR, _O_QS, _O_KS, _O_VS, _O_QF, _O_KF, _O_VF, _O_FG, IN_COLS_R = (
    0, 512, 768, 896, 1408, 1664, 1920, 2432, 2944, 3456, 3584)


def _chunk_pair(base_a, base_b):
    r = np.arange(HALF)
    return np.concatenate([base_a + r, base_b + r, base_a + HALF + r, base_b + HALF + r])


def _in_perm():
    src = [0, 512, 768, 832, 1344, 1472, 1600, 2112, 2624, 3136]
    zero = 3144
    cols = [np.arange(src[0], src[0] + 512), np.arange(src[1], src[1] + 256)]
    cols.append(_chunk_pair(src[2], src[2]))
    for c in range(SWA_HEADS // 2):
        cols.append(_chunk_pair(src[3] + SWA_DIM * 2 * c, src[3] + SWA_DIM * (2 * c + 1)))
    for g in range(SWA_KV_HEADS):
        cols.append(_chunk_pair(src[4] + SWA_DIM * g, src[4] + SWA_DIM * g))
    for g in range(SWA_KV_HEADS):
        v = src[5] + SWA_DIM * g + np.arange(SWA_DIM)
        cols.append(np.concatenate([v, v]))
    cols += [np.arange(src[6], src[6] + 512), np.arange(src[7], src[7] + 512), np.arange(src[8], src[8] + 512)]
    gate = np.tile(np.arange(src[9], src[9] + FOX_HEADS), GATE_COPIES)
    cols.append(np.concatenate([gate, np.full(LANES - gate.shape[0], zero)]))
    out = np.concatenate(cols).astype(np.int32)
    assert out.shape[0] == IN_COLS_R
    return out


def _qb_perm():
    per = MLA_NOPE + MLA_ROPE
    cols = [per * h + np.arange(MLA_NOPE) for h in range(MLA_HEADS)]
    for c in range(MLA_HEADS // 2):
        cols.append(_chunk_pair(per * 2 * c + MLA_NOPE, per * (2 * c + 1) + MLA_NOPE))
    return np.concatenate(cols).astype(np.int32)


def _kvb_perm():
    per = MLA_NOPE + MLA_V
    k = [per * h + np.arange(MLA_NOPE) for h in range(MLA_HEADS)]
    v = [per * h + MLA_NOPE + np.arange(MLA_V) for h in range(MLA_HEADS)]
    return np.concatenate(k + v).astype(np.int32)


_IN_PERM, _QB_PERM, _KVB_PERM = _in_perm(), _qb_perm(), _kvb_perm()


def _take_cols(w, perm):
    ncol = w.shape[-1]
    same_run = lambda k: (perm[k] == ncol) if perm[k - 1] == ncol else (perm[k] == perm[k - 1] + 1 and perm[k] != ncol)
    cuts = [0] + [k for k in range(1, len(perm)) if not same_run(k)] + [len(perm)]
    parts = []
    for a, b in zip(cuts[:-1], cuts[1:]):
        if perm[a] == ncol:
            parts.append(jnp.zeros(w.shape[:-1] + (b - a,), BF16))
        else:
            parts.append(w[..., int(perm[a]):int(perm[a]) + (b - a)].astype(BF16))
    return jnp.concatenate(parts, axis=-1)


def _params(*sem):
    return pltpu.CompilerParams(dimension_semantics=sem, vmem_limit_bytes=VMEM_LIMIT)


def _resident(block_shape, index_map):
    return pl.BlockSpec(block_shape, index_map, pipeline_mode=pl.Buffered(1))


def _rms(x, gain):
    ms = jnp.mean(x * x, axis=-1, keepdims=True)
    return x * lax.rsqrt(ms + RMS_EPS) * gain


def _ffn_kernel(x_ref, g_ref, wg_ref, wu_ref, wd_ref, fg_ref, o_ref, h_sc, *, final_norm):
    f = pl.program_id(1)

    def row_chunks(body):
        def step(r, carry):
            body(pl.ds(pl.multiple_of(r * FFN_ROW_CHUNK, FFN_ROW_CHUNK), FFN_ROW_CHUNK))
            return carry
        lax.fori_loop(0, x_ref.shape[0] // FFN_ROW_CHUNK, step, 0)

    @pl.when(f == 0)
    def _():
        def prologue(rows):
            h_sc[rows, :] = _rms(x_ref[rows, :], g_ref[...]).astype(BF16)
            o_ref[rows, :] = jnp.zeros((FFN_ROW_CHUNK, o_ref.shape[1]), F32)
        row_chunks(prologue)

    h = h_sc[...]
    gate = jnp.dot(h, wg_ref[...], preferred_element_type=F32)
    up = jnp.dot(h, wu_ref[...], preferred_element_type=F32)
    act = (gate * (1.0 / (1.0 + jnp.exp(-gate))) * up).astype(BF16)
    o_ref[...] += jnp.dot(act, wd_ref[...], preferred_element_type=F32)

    @pl.when(f == pl.num_programs(1) - 1)
    def _():
        def epilogue(rows):
            y = x_ref[rows, :] + 0.5 * o_ref[rows, :]
            if final_norm:
                y = _rms(y, fg_ref[...])
            o_ref[rows, :] = y
        row_chunks(epilogue)


def _ffn(x, gain, wg, wu, wd, layer, final_gain, *, final_norm, tm=1024, tf=512):
    T, D = x.shape
    F = wg.shape[-1]
    assert T % tm == 0 and F % tf == 0
    return pl.pallas_call(
        functools.partial(_ffn_kernel, final_norm=final_norm),
        grid=(T // tm, F // tf),
        in_specs=[
            pl.BlockSpec((tm, D), lambda i, f: (i, 0)),
            pl.BlockSpec((None, 1, D), lambda i, f: (layer, 0, 0)),
            pl.BlockSpec((None, D, tf), lambda i, f: (layer, 0, f)),
            pl.BlockSpec((None, D, tf), lambda i, f: (layer, 0, f)),
            pl.BlockSpec((None, tf, D), lambda i, f: (layer, f, 0)),
            pl.BlockSpec((1, D), lambda i, f: (0, 0)),
        ],
        out_specs=pl.BlockSpec((tm, D), lambda i, f: (i, 0)),
        out_shape=jax.ShapeDtypeStruct((T, D), F32),
        scratch_shapes=[pltpu.VMEM((tm, D), BF16)],
        compiler_params=_params("parallel", "arbitrary"),
        name="ffn",
    )(x, gain, wg, wu, wd, final_gain)


def _rope(x, cos, sin_signed):
    return x * cos + pltpu.roll(x, 2 * HALF, axis=1) * sin_signed


def _split3(x):
    a1 = x.astype(BF16)
    r1 = x - a1.astype(F32)
    a2 = r1.astype(BF16)
    a3 = (r1 - a2.astype(F32)).astype(BF16)
    return a1, a2, a3


def _store_chunks(ref, value):
    for c in range(ref.shape[0]):
        ref[c] = value[:, LANES * c:LANES * (c + 1)]


def _mixin_kernel(x_ref, g_ref, win_ref, qn_ref, wqb_ref, kvn_ref, wkvb_ref, cos_ref, sin_ref, fb_ref,
                  qmn_ref, qmp_ref, kmn_ref, kmr_ref, vm_ref, qs_ref, ks_ref, vs_ref,
                  qf_ref, kf_ref, vf_ref, qaug_ref, kaug_ref, carry_sc, *, steps_per_seq):
    i = pl.program_id(0)
    tm = x_ref.shape[0]

    @pl.when(i % steps_per_seq == 0)
    def _():
        carry_sc[...] = jnp.zeros_like(carry_sc)

    h = _rms(x_ref[...], g_ref[...]).astype(BF16)
    cos = cos_ref[...]
    sin = sin_ref[...]
    lane = lax.broadcasted_iota(jnp.int32, (1, LANES), 1)
    n_nope = MLA_HEADS * MLA_NOPE

    def proj(a, b):
        return jnp.dot(h, win_ref[:, a:b], preferred_element_type=F32)

    cq = _rms(proj(_O_CQ, _O_CKV), qn_ref[...]).astype(BF16)
    ckv = _rms(proj(_O_CKV, _O_KR), kvn_ref[...]).astype(BF16)
    fl = proj(_O_FG, IN_COLS_R) + fb_ref[...]
    logf = jnp.minimum(fl, 0.0) - jnp.log(1.0 + jnp.exp(-jnp.abs(fl)))

    kmr_ref[...] = _rope(proj(_O_KR, _O_QS), cos, sin).astype(BF16)
    qs = proj(_O_QS, _O_KS)
    for c in range(SWA_HEADS // 2):
        sl = slice(LANES * c, LANES * (c + 1))
        qs_ref[:, sl] = (_rope(qs[:, sl], cos, sin) * (SWA_DIM ** -0.5 * LOG2E)).astype(BF16)
    ks = proj(_O_KS, _O_VS)
    for g in range(SWA_KV_HEADS):
        sl = slice(LANES * g, LANES * (g + 1))
        ks_ref[:, sl] = _rope(ks[:, sl], cos, sin).astype(BF16)
    vs_ref[...] = proj(_O_VS, _O_QF).astype(BF16)

    _store_chunks(qf_ref, (proj(_O_QF, _O_KF) * (FOX_DIM ** -0.5 * LOG2E)).astype(BF16))
    _store_chunks(kf_ref, proj(_O_KF, _O_VF).astype(BF16))
    _store_chunks(vf_ref, proj(_O_VF, _O_FG).astype(BF16))

    row = lax.broadcasted_iota(jnp.int32, (tm, tm), 0)
    col = lax.broadcasted_iota(jnp.int32, (tm, tm), 1)
    tri = (col <= row).astype(BF16)
    csum = carry_sc[...]
    for term in _split3(logf):
        csum = csum + jnp.dot(tri, term, preferred_element_type=F32)
    carry_sc[...] = csum[tm - 1:tm, :]

    c1, c2, c3 = (term.astype(F32) for term in _split3(csum * LOG2E))
    g8 = lane // FOX_HEADS
    ones = jnp.where(g8 < GATE_COPIES, 1.0, 0.0)
    qaug_ref[...] = jnp.where(g8 < 3, ones, jnp.where(g8 == 3, c1, jnp.where(g8 == 4, c2, jnp.where(g8 == 5, c3, 0.0)))).astype(BF16)
    kaug_ref[...] = jnp.where(g8 == 0, -c1, jnp.where(g8 == 1, -c2, jnp.where(g8 == 2, -c3, ones))).astype(BF16)

    q = jnp.dot(cq, wqb_ref[...], preferred_element_type=F32)
    q_scale = (MLA_NOPE + MLA_ROPE) ** -0.5 * LOG2E
    _store_chunks(qmn_ref, (q[:, :n_nope] * q_scale).astype(BF16))
    for c in range(MLA_HEADS // 2):
        qmp_ref[c] = (_rope(q[:, n_nope + LANES * c:n_nope + LANES * (c + 1)], cos, sin) * q_scale).astype(BF16)
    kv = jnp.dot(ckv, wkvb_ref[...], preferred_element_type=F32)
    _store_chunks(kmn_ref, kv[:, :n_nope].astype(BF16))
    _store_chunks(vm_ref, kv[:, n_nope:].astype(BF16))


def _mixin(x, gain, w_in, qn, wqb, kvn, wkvb, cos, sin, fb, layer, *, batch, tm=256):
    T, D = x.shape
    S = T // batch
    assert S % tm == 0
    sps = S // tm

    def row(width):
        return pl.BlockSpec((tm, width), lambda i: (i, 0))

    def wres(a):
        return _resident((None,) + a.shape[1:], lambda i: (layer, 0, 0))

    def chunked(chunks):
        return (jax.ShapeDtypeStruct((batch, chunks, S, LANES), BF16),
                pl.BlockSpec((None, chunks, tm, LANES), lambda i: (i // sps, 0, i % sps, 0)))

    def flat(width):
        return jax.ShapeDtypeStruct((T, width), BF16), row(width)

    outs = dict(qmn=chunked(MLA_HEADS), qmp=chunked(MLA_HEADS // 2), kmn=chunked(MLA_HEADS), kmr=flat(LANES),
                vm=chunked(MLA_HEADS), qs=flat(SWA_DIM * SWA_HEADS), ks=flat(LANES * SWA_KV_HEADS),
                vs=flat(LANES * SWA_KV_HEADS), qf=chunked(FOX_HEADS // 2), kf=chunked(FOX_HEADS // 2),
                vf=chunked(FOX_HEADS // 2), qaug=flat(LANES), kaug=flat(LANES))
    return pl.pallas_call(
        functools.partial(_mixin_kernel, steps_per_seq=sps),
        grid=(T // tm,),
        in_specs=[row(D), wres(gain), wres(w_in), wres(qn), wres(wqb), wres(kvn), wres(wkvb),
                  row(LANES), row(LANES), wres(fb)],
        out_specs=[spec for _, spec in outs.values()],
        out_shape=[shape for shape, _ in outs.values()],
        scratch_shapes=[pltpu.VMEM((1, LANES), F32)],
        compiler_params=_params("arbitrary"),
        name="mixin",
    )(x, gain, w_in, qn, wqb, kvn, wkvb, cos, sin, fb)


def _qk(q, k):
    return lax.dot_general(q, k, (((1,), (1,)), ((), ())), preferred_element_type=F32)


def _causal_mask(t):
    row = lax.broadcasted_iota(jnp.int32, (t, t), 0)
    col = lax.broadcasted_iota(jnp.int32, (t, t), 1)
    return col <= row


def _flash_kernel(tab_ref, qa_ref, qb_ref, ka_ref, kb_ref, v_ref, o_ref,
                  q_sc, k_sc, v_sc, m_sc, acc_sc, *bufs, kind, t, n_off):
    h = pl.program_id(1)
    n = q_sc.shape[0] // t
    lane = lax.broadcasted_iota(jnp.int32, (1, LANES), 1)
    even = (h % 2) == 0
    zero = jnp.zeros((), BF16)

    if kind == "mla":
        mine = ((lane % (2 * HALF)) < HALF) == even
        q_sc[:, :LANES] = qa_ref[...]
        q_sc[:, LANES:] = qb_ref[...]
        k_sc[:, :LANES] = ka_ref[...]
        k_sc[:, LANES:] = jnp.where(mine, kb_ref[...], zero)
        v_sc[:, :LANES] = v_ref[...]
    else:
        mine = (lane < FOX_DIM) == even
        gate = ((lane % FOX_HEADS) == h) & (lane < GATE_COPIES * FOX_HEADS)
        q_sc[:, :LANES] = qa_ref[...]
        q_sc[:, LANES:] = jnp.where(gate, qb_ref[...], zero)
        k_sc[:, :LANES] = jnp.where(mine, ka_ref[...], zero)
        k_sc[:, LANES:] = kb_ref[...]
        v_sc[:, :LANES] = jnp.where(mine, v_ref[...], zero)
    v_sc[:, LANES:] = jnp.ones((v_sc.shape[0], LANES), BF16)

    def rows(j):
        return pl.ds(pl.multiple_of(j * t, t), t)

    s_bufs, p_bufs, a_bufs, r_bufs = (bufs[FLASH_BUFS * g:FLASH_BUFS * (g + 1)] for g in range(4))

    def scores(i, j, slot):
        s = _qk(q_sc[rows(i), :], k_sc[rows(j), :])
        s_bufs[slot][...] = s
        r_bufs[slot][...] = functools.reduce(jnp.maximum, [s[:, LANES * c:LANES * (c + 1)] for c in range(t // LANES)])

    def softmax(par, i, first):
        s = s_bufs[par][...]
        if first:
            s = jnp.where(_causal_mask(t), s, NEG)
        tiles = [s[:, LANES * c:LANES * (c + 1)] for c in range(t // LANES)]
        tile_max = functools.reduce(jnp.maximum, tiles) if first else r_bufs[par][...]
        m_new = jnp.broadcast_to(jnp.max(tile_max, axis=-1, keepdims=True), (t, LANES))
        if not first:
            m_old = m_sc[rows(i), :]
            m_new = jnp.maximum(m_old, m_new)
            a_bufs[par][...] = jnp.exp2(m_old - m_new)
        for c, tile in enumerate(tiles):
            p_bufs[par][:, LANES * c:LANES * (c + 1)] = jnp.exp2(tile - m_new).astype(BF16)
        m_sc[rows(i), :] = m_new

    def accumulate(par, i, j, first):
        pv = jnp.dot(p_bufs[par][...], v_sc[rows(j), :], preferred_element_type=F32)
        if first:
            acc_sc[rows(i), :] = pv
        else:
            a = a_bufs[par][...]
            acc_sc[rows(i), :] = jnp.concatenate([a, a], axis=1) * acc_sc[rows(i), :] + pv

    def pipeline(base, count, first, lag, unroll=1):
        nb = 2 * lag
        blk = lambda a: (tab_ref[0, base + a], tab_ref[1, base + a])

        def step(a, u, with_acc=True):
            if with_acc:
                accumulate((u - lag) % nb, *blk(a - lag), first)
            scores(*blk(a + lag), (u + lag) % nb)
            softmax(u, blk(a)[0], first)

        for a in range(lag):
            scores(*blk(a), a)
        for a in range(min(lag, count)):
            step(a, a, with_acc=False)
        main = max(count - lag, 0)

        span = nb * unroll

        def body(k, carry):
            for u in range(span):
                step(lag + span * k + u, (lag + u) % nb)
            return carry

        lax.fori_loop(0, main // span, body, 0)
        for a in range(lag + main // span * span, count):
            step(a, a % nb)
        for a in range(main, count):
            accumulate(a % nb, *blk(a), first)

    pipeline(0, n, True, 1)
    pipeline(n, n_off, False, FLASH_LAG, unroll=FLASH_UNROLL)

    def finish(i, carry):
        o_ref[rows(i), :] = (acc_sc[rows(i), :LANES] / acc_sc[rows(i), LANES:]).astype(o_ref.dtype)
        return carry

    lax.fori_loop(0, n, finish, 0)


def _flash_attention(qa, qb, ka, kb, v, *, kind, t=512):
    B, _, S, _ = qa.shape
    heads = MLA_HEADS if kind == "mla" else FOX_HEADS
    n = S // t
    assert S % t == 0 and n % 2 == 0
    diag = [(i, i) for i in range(n)]
    off = [(i, j) for j in range(n - 1) for i in range(j + 1, n)]
    table = jnp.asarray(np.array(diag + off + off[-1:] * (2 * FLASH_LAG), np.int32).T)
    shared = lambda **kw: pl.BlockSpec((None, S, LANES), lambda b, h: (b, 0, 0), **kw)
    own = lambda **kw: pl.BlockSpec((None, None, S, LANES), lambda b, h: (b, h, 0, 0), **kw)
    pair = lambda **kw: pl.BlockSpec((None, None, S, LANES), lambda b, h: (b, h // 2, 0, 0), **kw)
    single = dict(pipeline_mode=pl.Buffered(1))
    if kind == "mla":
        in_specs = [own(**single), pair(**single), own(), shared(), own()]
    else:
        in_specs = [pair(**single), shared(**single), pair(), shared(), pair()]
    return pl.pallas_call(
        functools.partial(_flash_kernel, kind=kind, t=t, n_off=len(off)),
        grid=(B, heads),
        in_specs=[pl.BlockSpec(memory_space=pltpu.SMEM)] + in_specs,
        out_specs=own(),
        out_shape=jax.ShapeDtypeStruct((B, heads, S, LANES), BF16),
        scratch_shapes=[pltpu.VMEM((S, 2 * LANES), BF16), pltpu.VMEM((S, 2 * LANES), BF16),
                        pltpu.VMEM((S, 2 * LANES), BF16),
                        pltpu.VMEM((S, LANES), F32), pltpu.VMEM((S, 2 * LANES), F32)]
                       + [pltpu.VMEM((t, t), F32)] * FLASH_BUFS
                       + [pltpu.VMEM((t, t), BF16)] * FLASH_BUFS
                       + [pltpu.VMEM((t, LANES), F32)] * FLASH_BUFS
                       + [pltpu.VMEM((t, LANES), F32)] * FLASH_BUFS,
        compiler_params=_params("parallel", "arbitrary"),
        name=kind + "_attn",
    )(table, qa, qb, ka, kb, v)


def _swa_kernel(sink_ref, q_ref, kc_ref, kp_ref, vc_ref, vp_ref, o_ref):
    i = pl.program_id(1)
    t = q_ref.shape[0]
    lane = lax.broadcasted_iota(jnp.int32, (1, LANES), 1)
    pair_lo = (lane % (2 * HALF)) < HALF
    k_all = jnp.concatenate([kp_ref[...], kc_ref[...]], axis=0)
    v_all = jnp.concatenate([vp_ref[...], vc_ref[...]], axis=0)
    row = lax.broadcasted_iota(jnp.int32, (t, WINDOW + t), 0)
    col = lax.broadcasted_iota(jnp.int32, (t, WINDOW + t), 1)
    valid = (col <= row + WINDOW) & (col > row) & ((col >= WINDOW) | (i > 0))
    group = SWA_HEADS // SWA_KV_HEADS
    zero = jnp.zeros((), BF16)
    kv_of = lambda hh: hh // group
    scores = []
    for hh in range(SWA_HEADS):
        kg = k_all[:, LANES * kv_of(hh):LANES * (kv_of(hh) + 1)]
        ke = jnp.where(pair_lo, kg, zero) if hh % 2 == 0 else jnp.where(pair_lo, zero, kg)
        scores.append(_qk(q_ref[:, LANES * (hh // 2):LANES * (hh // 2 + 1)], ke))
    probs, denoms = [], []
    for hh in range(SWA_HEADS):
        sink = sink_ref[hh] * LOG2E
        s = jnp.where(valid, scores[hh], NEG)
        m = jnp.maximum(jnp.max(s, axis=-1, keepdims=True), sink)
        p = jnp.exp2(s - m)
        denoms.append(jnp.sum(p, axis=-1, keepdims=True) + jnp.exp2(sink - m))
        probs.append(p.astype(BF16))
    outs = [jnp.dot(probs[hh], v_all[:, LANES * kv_of(hh):LANES * (kv_of(hh) + 1)], preferred_element_type=F32) / denoms[hh]
            for hh in range(SWA_HEADS)]
    for c in range(SWA_HEADS // 2):
        o_ref[:, LANES * c:LANES * (c + 1)] = jnp.where(lane < SWA_DIM, outs[2 * c], outs[2 * c + 1]).astype(o_ref.dtype)


def _swa_attention(sinks, q, k, v, *, t=256):
    B, S, _ = q.shape
    assert S % t == 0 and t % WINDOW == 0
    r = t // WINDOW
    kv_w = LANES * SWA_KV_HEADS
    cur = pl.BlockSpec((None, t, kv_w), lambda b, i: (b, i, 0))
    prev = pl.BlockSpec((None, WINDOW, kv_w), lambda b, i: (b, jnp.maximum(i * r - 1, 0), 0))
    return pl.pallas_call(
        _swa_kernel,
        grid=(B, S // t),
        in_specs=[
            pl.BlockSpec(memory_space=pltpu.SMEM),
            pl.BlockSpec((None, t, SWA_HEADS * SWA_DIM), lambda b, i: (b, i, 0)),
            cur, prev, cur, prev,
        ],
        out_specs=pl.BlockSpec((None, t, SWA_HEADS * SWA_DIM), lambda b, i: (b, i, 0)),
        out_shape=jax.ShapeDtypeStruct((B, S, SWA_HEADS * SWA_DIM), BF16),
        compiler_params=_params("parallel", "arbitrary"),
        name="swa_attn",
    )(sinks, q, k, k, v, v)


def _outproj_kernel(x_ref, om_ref, os_ref, of_ref, w_ref, o_ref):
    a = om_ref.shape[0] * LANES
    b = a + os_ref.shape[1]
    om = jnp.concatenate([om_ref[h] for h in range(om_ref.shape[0])], axis=1)
    y = jnp.dot(om, w_ref[:a, :], preferred_element_type=F32)
    y += jnp.dot(os_ref[...], w_ref[a:b, :], preferred_element_type=F32)
    of = jnp.concatenate([of_ref[2 * c] + of_ref[2 * c + 1] for c in range(of_ref.shape[0] // 2)], axis=1)
    y += jnp.dot(of, w_ref[b:, :], preferred_element_type=F32)
    o_ref[...] = x_ref[...] + y


def _outproj(x, om, osw, of, w_out, layer, *, tm=512):
    T, D = x.shape
    S = om.shape[2]
    assert S % tm == 0
    sps = S // tm

    def row(width):
        return pl.BlockSpec((tm, width), lambda i: (i, 0))

    def heads(a):
        return pl.BlockSpec((None, a.shape[1], tm, LANES), lambda i: (i // sps, 0, i % sps, 0))

    return pl.pallas_call(
        _outproj_kernel,
        grid=(T // tm,),
        in_specs=[row(D), heads(om), row(osw.shape[1]), heads(of),
                  _resident((None,) + w_out.shape[1:], lambda i: (layer, 0, 0))],
        out_specs=row(D),
        out_shape=jax.ShapeDtypeStruct((T, D), F32),
        compiler_params=_params("parallel"),
        name="outproj",
    )(x, om, osw, of, w_out)


def kernel(x, positions, ffn1_norm, ffn1_w_gate, ffn1_w_up, ffn1_w_down, mix_norm, w_in, mla_q_norm, mla_w_q_b, mla_kv_norm, mla_w_kv_b, swa_sinks, fox_forget_bias, w_out, ffn2_norm, ffn2_w_gate, ffn2_w_up, ffn2_w_down, final_norm):
    B, S, D = x.shape
    depth = w_in.shape[0]
    T = B * S

    bf = lambda w: w.astype(BF16)
    w_in_r = _take_cols(w_in, _IN_PERM)
    wqb_r = _take_cols(mla_w_q_b, _QB_PERM)
    wkvb_r = _take_cols(mla_w_kv_b, _KVB_PERM)
    ffn_w = [(bf(ffn1_w_gate), bf(ffn1_w_up), bf(ffn1_w_down)), (bf(ffn2_w_gate), bf(ffn2_w_up), bf(ffn2_w_down))]
    ffn_g = [ffn1_norm[:, None, :], ffn2_norm[:, None, :]]
    w_out_b = bf(w_out)
    mix_g = mix_norm[:, None, :]
    qn = mla_q_norm[:, None, :]
    kvn = mla_kv_norm[:, None, :]
    fb = jnp.pad(jnp.tile(fox_forget_bias, (1, GATE_COPIES)), ((0, 0), (0, LANES - GATE_COPIES * FOX_HEADS)))[:, None, :]
    fin_g = final_norm[None, :]

    inv_freq = ROPE_THETA ** (-jnp.arange(0, 2 * HALF, 2, dtype=F32) / (2 * HALF))
    ang = positions.astype(F32).reshape(T, 1) * inv_freq
    cos, sin = jnp.cos(ang), jnp.sin(ang)
    cos_t = jnp.concatenate([cos, cos, cos, cos], axis=-1)
    sin_t = jnp.concatenate([-sin, -sin, sin, sin], axis=-1)

    xt = x.reshape(T, D)
    sh = lambda a: a.reshape(B, S, a.shape[-1])
    for l in range(depth):
        xt = _ffn(xt, ffn_g[0], *ffn_w[0], l, fin_g, final_norm=False)
        qmn, qmp, kmn, kmr, vm, qs, ks, vs, qf, kf, vf, qaug, kaug = _mixin(
            xt, mix_g, w_in_r, qn, wqb_r, kvn, wkvb_r, cos_t, sin_t, fb, l, batch=B)
        o_mla = _flash_attention(qmn, qmp, kmn, sh(kmr), vm, kind="mla")
        o_swa = _swa_attention(swa_sinks[l], sh(qs), sh(ks), sh(vs))
        o_fox = _flash_attention(qf, sh(qaug), kf, sh(kaug), vf, kind="fox")
        xt = _outproj(xt, o_mla, o_swa.reshape(T, -1), o_fox, w_out_b, l)
        xt = _ffn(xt, ffn_g[1], *ffn_w[1], l, fin_g, final_norm=(l == depth - 1))
    return xt.reshape(B, S, D)
```

```python
import functools
import math

import numpy as np
import jax
import jax.numpy as jnp
from jax import lax
from jax.experimental import pallas as pl
from jax.experimental.pallas import tpu as pltpu

RMS_EPS = 1e-6
ROPE_THETA = 10000.0

MLA_HEADS = 8
MLA_Q_LORA = 512
MLA_KV_LORA = 256
MLA_NOPE = 128
MLA_ROPE = 64
MLA_V = 128

SWA_HEADS = 8
SWA_KV_HEADS = 2
SWA_DIM = 64
WINDOW = 128

FOX_HEADS = 8
FOX_DIM = 64

LANES = 128
HALF = 32
VMEM_LIMIT = 60 * 1024 * 1024
FFN_ROW_CHUNK = 256
FLASH_LAG = 2
FLASH_BUFS = 2 * FLASH_LAG
FLASH_UNROLL = 7
NEG = -1e30
LOG2E = math.log2(math.e)
GATE_COPIES = 6

F32 = jnp.float32
BF16 = jnp.bfloat16

_O_CQ, _O_CKV, _O_KR, _O_QS, _O_KS, _O_VS, _O_QF, _O_KF, _O_VF, _O_FG, IN_COLS_R = (
    0, 512, 768, 896, 1408, 1664, 1920, 2432, 2944, 3456, 3584)


def _chunk_pair(base_a, base_b):
    r = np.arange(HALF)
    return np.concatenate([base_a + r, base_b + r, base_a + HALF + r, base_b + HALF + r])


def _in_perm():
    src = [0, 512, 768, 832, 1344, 1472, 1600, 2112, 2624, 3136]
    zero = 3144
    cols = [np.arange(src[0], src[0] + 512), np.arange(src[1], src[1] + 256)]
    cols.append(_chunk_pair(src[2], src[2]))
    for c in range(SWA_HEADS // 2):
        cols.append(_chunk_pair(src[3] + SWA_DIM * 2 * c, src[3] + SWA_DIM * (2 * c + 1)))
    for g in range(SWA_KV_HEADS):
        cols.append(_chunk_pair(src[4] + SWA_DIM * g, src[4] + SWA_DIM * g))
    for g in range(SWA_KV_HEADS):
        v = src[5] + SWA_DIM * g + np.arange(SWA_DIM)
        cols.append(np.concatenate([v, v]))
    cols += [np.arange(src[6], src[6] + 512), np.arange(src[7], src[7] + 512), np.arange(src[8], src[8] + 512)]
    gate = np.tile(np.arange(src[9], src[9] + FOX_HEADS), GATE_COPIES)
    cols.append(np.concatenate([gate, np.full(LANES - gate.shape[0], zero)]))
    out = np.concatenate(cols).astype(np.int32)
    assert out.shape[0] == IN_COLS_R
    return out


def _qb_perm():
    per = MLA_NOPE + MLA_ROPE
    cols = [per * h + np.arange(MLA_NOPE) for h in range(MLA_HEADS)]
    for c in range(MLA_HEADS // 2):
        cols.append(_chunk_pair(per * 2 * c + MLA_NOPE, per * (2 * c + 1) + MLA_NOPE))
    return np.concatenate(cols).astype(np.int32)


def _kvb_perm():
    per = MLA_NOPE + MLA_V
    k = [per * h + np.arange(MLA_NOPE) for h in range(MLA_HEADS)]
    v = [per * h + MLA_NOPE + np.arange(MLA_V) for h in range(MLA_HEADS)]
    return np.concatenate(k + v).astype(np.int32)


_IN_PERM, _QB_PERM, _KVB_PERM = _in_perm(), _qb_perm(), _kvb_perm()


def _take_cols(w, perm):
    ncol = w.shape[-1]
    same_run = lambda k: (perm[k] == ncol) if perm[k - 1] == ncol else (perm[k] == perm[k - 1] + 1 and perm[k] != ncol)
    cuts = [0] + [k for k in range(1, len(perm)) if not same_run(k)] + [len(perm)]
    parts = []
    for a, b in zip(cuts[:-1], cuts[1:]):
        if perm[a] == ncol:
            parts.append(jnp.zeros(w.shape[:-1] + (b - a,), w.dtype))
        else:
            parts.append(w[..., int(perm[a]):int(perm[a]) + (b - a)])
    return jnp.concatenate(parts, axis=-1).astype(BF16)


def _params(*sem):
    return pltpu.CompilerParams(dimension_semantics=sem, vmem_limit_bytes=VMEM_LIMIT)


def _resident(block_shape, index_map):
    return pl.BlockSpec(block_shape, index_map, pipeline_mode=pl.Buffered(1))


def _rms(x, gain):
    ms = jnp.mean(x * x, axis=-1, keepdims=True)
    return x * lax.rsqrt(ms + RMS_EPS) * gain


def _ffn_kernel(x_ref, g_ref, wg_ref, wu_ref, wd_ref, fg_ref, o_ref, h_sc, *, final_norm):
    f = pl.program_id(1)

    def row_chunks(body):
        def step(r, carry):
            body(pl.ds(pl.multiple_of(r * FFN_ROW_CHUNK, FFN_ROW_CHUNK), FFN_ROW_CHUNK))
            return carry
        lax.fori_loop(0, x_ref.shape[0] // FFN_ROW_CHUNK, step, 0)

    @pl.when(f == 0)
    def _():
        def prologue(rows):
            h_sc[rows, :] = _rms(x_ref[rows, :], g_ref[...]).astype(BF16)
            o_ref[rows, :] = jnp.zeros((FFN_ROW_CHUNK, o_ref.shape[1]), F32)
        row_chunks(prologue)

    h = h_sc[...]
    gate = jnp.dot(h, wg_ref[...], preferred_element_type=F32)
    up = jnp.dot(h, wu_ref[...], preferred_element_type=F32)
    act = (gate * (1.0 / (1.0 + jnp.exp(-gate))) * up).astype(BF16)
    o_ref[...] += jnp.dot(act, wd_ref[...], preferred_element_type=F32)

    @pl.when(f == pl.num_programs(1) - 1)
    def _():
        def epilogue(rows):
            y = x_ref[rows, :] + 0.5 * o_ref[rows, :]
            if final_norm:
                y = _rms(y, fg_ref[...])
            o_ref[rows, :] = y
        row_chunks(epilogue)


def _ffn(x, gain, wg, wu, wd, layer, final_gain, *, final_norm, tm=1024, tf=512):
    T, D = x.shape
    F = wg.shape[-1]
    assert T % tm == 0 and F % tf == 0
    return pl.pallas_call(
        functools.partial(_ffn_kernel, final_norm=final_norm),
        grid=(T // tm, F // tf),
        in_specs=[
            pl.BlockSpec((tm, D), lambda i, f: (i, 0)),
            pl.BlockSpec((None, 1, D), lambda i, f: (layer, 0, 0)),
            pl.BlockSpec((None, D, tf), lambda i, f: (layer, 0, f)),
            pl.BlockSpec((None, D, tf), lambda i, f: (layer, 0, f)),
            pl.BlockSpec((None, tf, D), lambda i, f: (layer, f, 0)),
            pl.BlockSpec((1, D), lambda i, f: (0, 0)),
        ],
        out_specs=pl.BlockSpec((tm, D), lambda i, f: (i, 0)),
        out_shape=jax.ShapeDtypeStruct((T, D), F32),
        scratch_shapes=[pltpu.VMEM((tm, D), BF16)],
        compiler_params=_params("parallel", "arbitrary"),
        name="ffn",
    )(x, gain, wg, wu, wd, final_gain)


def _rope(x, cos, sin_signed):
    return x * cos + pltpu.roll(x, 2 * HALF, axis=1) * sin_signed


def _split3(x):
    a1 = x.astype(BF16)
    r1 = x - a1.astype(F32)
    a2 = r1.astype(BF16)
    a3 = (r1 - a2.astype(F32)).astype(BF16)
    return a1, a2, a3


def _store_chunks(ref, value):
    for c in range(ref.shape[0]):
        ref[c] = value[:, LANES * c:LANES * (c + 1)]


def _mixin_kernel(x_ref, g_ref, win_ref, qn_ref, wqb_ref, kvn_ref, wkvb_ref, cos_ref, sin_ref, fb_ref,
                  qmn_ref, qmp_ref, kmn_ref, kmr_ref, vm_ref, qs_ref, ks_ref, vs_ref,
                  qf_ref, kf_ref, vf_ref, qaug_ref, kaug_ref, carry_sc, *, steps_per_seq):
    i = pl.program_id(0)
    tm = x_ref.shape[0]

    @pl.when(i % steps_per_seq == 0)
    def _():
        carry_sc[...] = jnp.zeros_like(carry_sc)

    h = _rms(x_ref[...], g_ref[...]).astype(BF16)
    cos = cos_ref[...]
    sin = sin_ref[...]
    lane = lax.broadcasted_iota(jnp.int32, (1, LANES), 1)
    n_nope = MLA_HEADS * MLA_NOPE

    def proj(a, b):
        return jnp.dot(h, win_ref[:, a:b], preferred_element_type=F32)

    cq = _rms(proj(_O_CQ, _O_CKV), qn_ref[...]).astype(BF16)
    ckv = _rms(proj(_O_CKV, _O_KR), kvn_ref[...]).astype(BF16)
    fl = proj(_O_FG, IN_COLS_R) + fb_ref[...]
    logf = jnp.minimum(fl, 0.0) - jnp.log(1.0 + jnp.exp(-jnp.abs(fl)))

    kmr_ref[...] = _rope(proj(_O_KR, _O_QS), cos, sin).astype(BF16)
    qs = proj(_O_QS, _O_KS)
    for c in range(SWA_HEADS // 2):
        sl = slice(LANES * c, LANES * (c + 1))
        qs_ref[:, sl] = (_rope(qs[:, sl], cos, sin) * (SWA_DIM ** -0.5 * LOG2E)).astype(BF16)
    ks = proj(_O_KS, _O_VS)
    for g in range(SWA_KV_HEADS):
        sl = slice(LANES * g, LANES * (g + 1))
        ks_ref[:, sl] = _rope(ks[:, sl], cos, sin).astype(BF16)
    vs_ref[...] = proj(_O_VS, _O_QF).astype(BF16)

    _store_chunks(qf_ref, (proj(_O_QF, _O_KF) * (FOX_DIM ** -0.5 * LOG2E)).astype(BF16))
    _store_chunks(kf_ref, proj(_O_KF, _O_VF).astype(BF16))
    _store_chunks(vf_ref, proj(_O_VF, _O_FG).astype(BF16))

    row = lax.broadcasted_iota(jnp.int32, (tm, tm), 0)
    col = lax.broadcasted_iota(jnp.int32, (tm, tm), 1)
    tri = (col <= row).astype(BF16)
    csum = carry_sc[...]
    for term in _split3(logf):
        csum = csum + jnp.dot(tri, term, preferred_element_type=F32)
    carry_sc[...] = csum[tm - 1:tm, :]

    c1, c2, c3 = (term.astype(F32) for term in _split3(csum * LOG2E))
    g8 = lane // FOX_HEADS
    ones = jnp.where(g8 < GATE_COPIES, 1.0, 0.0)
    qaug_ref[...] = jnp.where(g8 < 3, ones, jnp.where(g8 == 3, c1, jnp.where(g8 == 4, c2, jnp.where(g8 == 5, c3, 0.0)))).astype(BF16)
    kaug_ref[...] = jnp.where(g8 == 0, -c1, jnp.where(g8 == 1, -c2, jnp.where(g8 == 2, -c3, ones))).astype(BF16)

    q = jnp.dot(cq, wqb_ref[...], preferred_element_type=F32)
    q_scale = (MLA_NOPE + MLA_ROPE) ** -0.5 * LOG2E
    _store_chunks(qmn_ref, (q[:, :n_nope] * q_scale).astype(BF16))
    for c in range(MLA_HEADS // 2):
        qmp_ref[c] = (_rope(q[:, n_nope + LANES * c:n_nope + LANES * (c + 1)], cos, sin) * q_scale).astype(BF16)
    kv = jnp.dot(ckv, wkvb_ref[...], preferred_element_type=F32)
    _store_chunks(kmn_ref, kv[:, :n_nope].astype(BF16))
    _store_chunks(vm_ref, kv[:, n_nope:].astype(BF16))


def _mixin(x, gain, w_in, qn, wqb, kvn, wkvb, cos, sin, fb, layer, *, batch, tm=256):
    T, D = x.shape
    S = T // batch
    assert S % tm == 0
    sps = S // tm

    def row(width):
        return pl.BlockSpec((tm, width), lambda i: (i, 0))

    def wres(a):
        return _resident((None,) + a.shape[1:], lambda i: (layer, 0, 0))

    def chunked(chunks):
        return (jax.ShapeDtypeStruct((batch, chunks, S, LANES), BF16),
                pl.BlockSpec((None, chunks, tm, LANES), lambda i: (i // sps, 0, i % sps, 0)))

    def flat(width):
        return jax.ShapeDtypeStruct((T, width), BF16), row(width)

    outs = dict(qmn=chunked(MLA_HEADS), qmp=chunked(MLA_HEADS // 2), kmn=chunked(MLA_HEADS), kmr=flat(LANES),
                vm=chunked(MLA_HEADS), qs=flat(SWA_DIM * SWA_HEADS), ks=flat(LANES * SWA_KV_HEADS),
                vs=flat(LANES * SWA_KV_HEADS), qf=chunked(FOX_HEADS // 2), kf=chunked(FOX_HEADS // 2),
                vf=chunked(FOX_HEADS // 2), qaug=flat(LANES), kaug=flat(LANES))
    return pl.pallas_call(
        functools.partial(_mixin_kernel, steps_per_seq=sps),
        grid=(T // tm,),
        in_specs=[row(D), wres(gain), wres(w_in), wres(qn), wres(wqb), wres(kvn), wres(wkvb),
                  row(LANES), row(LANES), wres(fb)],
        out_specs=[spec for _, spec in outs.values()],
        out_shape=[shape for shape, _ in outs.values()],
        scratch_shapes=[pltpu.VMEM((1, LANES), F32)],
        compiler_params=_params("arbitrary"),
        name="mixin",
    )(x, gain, w_in, qn, wqb, kvn, wkvb, cos, sin, fb)


def _qk(q, k):
    return lax.dot_general(q, k, (((1,), (1,)), ((), ())), preferred_element_type=F32)


def _causal_mask(t):
    row = lax.broadcasted_iota(jnp.int32, (t, t), 0)
    col = lax.broadcasted_iota(jnp.int32, (t, t), 1)
    return col <= row


def _flash_kernel(tab_ref, qa_ref, qb_ref, ka_ref, kb_ref, v_ref, o_ref,
                  q_sc, k_sc, v_sc, m_sc, acc_sc, *bufs, kind, t, n_off):
    h = pl.program_id(1)
    n = q_sc.shape[0] // t
    lane = lax.broadcasted_iota(jnp.int32, (1, LANES), 1)
    even = (h % 2) == 0
    zero = jnp.zeros((), BF16)

    if kind == "mla":
        mine = ((lane % (2 * HALF)) < HALF) == even
        q_sc[:, :LANES] = qa_ref[...]
        q_sc[:, LANES:] = qb_ref[...]
        k_sc[:, :LANES] = ka_ref[...]
        k_sc[:, LANES:] = jnp.where(mine, kb_ref[...], zero)
        v_sc[:, :LANES] = v_ref[...]
    else:
        mine = (lane < FOX_DIM) == even
        gate = ((lane % FOX_HEADS) == h) & (lane < GATE_COPIES * FOX_HEADS)
        q_sc[:, :LANES] = qa_ref[...]
        q_sc[:, LANES:] = jnp.where(gate, qb_ref[...], zero)
        k_sc[:, :LANES] = jnp.where(mine, ka_ref[...], zero)
        k_sc[:, LANES:] = kb_ref[...]
        v_sc[:, :LANES] = jnp.where(mine, v_ref[...], zero)
    v_sc[:, LANES:] = jnp.ones((v_sc.shape[0], LANES), BF16)

    def rows(j):
        return pl.ds(pl.multiple_of(j * t, t), t)

    s_bufs, p_bufs, a_bufs, r_bufs = (bufs[FLASH_BUFS * g:FLASH_BUFS * (g + 1)] for g in range(4))

    def scores(i, j, slot):
        s = _qk(q_sc[rows(i), :], k_sc[rows(j), :])
        s_bufs[slot][...] = s
        r_bufs[slot][...] = functools.reduce(jnp.maximum, [s[:, LANES * c:LANES * (c + 1)] for c in range(t // LANES)])

    def softmax(par, i, first):
        s = s_bufs[par][...]
        if first:
            s = jnp.where(_causal_mask(t), s, NEG)
        tiles = [s[:, LANES * c:LANES * (c + 1)] for c in range(t // LANES)]
        tile_max = functools.reduce(jnp.maximum, tiles) if first else r_bufs[par][...]
        m_new = jnp.broadcast_to(jnp.max(tile_max, axis=-1, keepdims=True), (t, LANES))
        if not first:
            m_old = m_sc[rows(i), :]
            m_new = jnp.maximum(m_old, m_new)
            a_bufs[par][...] = jnp.exp2(m_old - m_new)
        for c, tile in enumerate(tiles):
            p_bufs[par][:, LANES * c:LANES * (c + 1)] = jnp.exp2(tile - m_new).astype(BF16)
        m_sc[rows(i), :] = m_new

    def accumulate(par, i, j, first):
        pv = jnp.dot(p_bufs[par][...], v_sc[rows(j), :], preferred_element_type=F32)
        if first:
            acc_sc[rows(i), :] = pv
        else:
            a = a_bufs[par][...]
            acc_sc[rows(i), :] = jnp.concatenate([a, a], axis=1) * acc_sc[rows(i), :] + pv

    def pipeline(base, count, first, lag, unroll=1):
        nb = 2 * lag
        blk = lambda a: (tab_ref[0, base + a], tab_ref[1, base + a])

        def step(a, u, with_acc=True):
            if with_acc:
                accumulate((u - lag) % nb, *blk(a - lag), first)
            scores(*blk(a + lag), (u + lag) % nb)
            softmax(u, blk(a)[0], first)

        for a in range(lag):
            scores(*blk(a), a)
        for a in range(min(lag, count)):
            step(a, a, with_acc=False)
        main = max(count - lag, 0)

        span = nb * unroll

        def body(k, carry):
            for u in range(span):
                step(lag + span * k + u, (lag + u) % nb)
            return carry

        lax.fori_loop(0, main // span, body, 0)
        for a in range(lag + main // span * span, count):
            step(a, a % nb)
        for a in range(main, count):
            accumulate(a % nb, *blk(a), first)

    pipeline(0, n, True, 1)
    pipeline(n, n_off, False, FLASH_LAG, unroll=FLASH_UNROLL)

    def finish(i, carry):
        o_ref[rows(i), :] = (acc_sc[rows(i), :LANES] / acc_sc[rows(i), LANES:]).astype(o_ref.dtype)
        return carry

    lax.fori_loop(0, n, finish, 0)


def _flash_attention(qa, qb, ka, kb, v, *, kind, t=512):
    B, _, S, _ = qa.shape
    heads = MLA_HEADS if kind == "mla" else FOX_HEADS
    n = S // t
    assert S % t == 0 and n % 2 == 0
    diag = [(i, i) for i in range(n)]
    off = [(i, j) for j in range(n - 1) for i in range(j + 1, n)]
    table = jnp.asarray(np.array(diag + off + off[-1:] * (2 * FLASH_LAG), np.int32).T)
    shared = lambda **kw: pl.BlockSpec((None, S, LANES), lambda b, h: (b, 0, 0), **kw)
    own = lambda **kw: pl.BlockSpec((None, None, S, LANES), lambda b, h: (b, h, 0, 0), **kw)
    pair = lambda **kw: pl.BlockSpec((None, None, S, LANES), lambda b, h: (b, h // 2, 0, 0), **kw)
    single = dict(pipeline_mode=pl.Buffered(1))
    if kind == "mla":
        in_specs = [own(**single), pair(**single), own(), shared(), own()]
    else:
        in_specs = [pair(**single), shared(**single), pair(), shared(), pair()]
    return pl.pallas_call(
        functools.partial(_flash_kernel, kind=kind, t=t, n_off=len(off)),
        grid=(B, heads),
        in_specs=[pl.BlockSpec(memory_space=pltpu.SMEM)] + in_specs,
        out_specs=own(),
        out_shape=jax.ShapeDtypeStruct((B, heads, S, LANES), BF16),
        scratch_shapes=[pltpu.VMEM((S, 2 * LANES), BF16), pltpu.VMEM((S, 2 * LANES), BF16),
                        pltpu.VMEM((S, 2 * LANES), BF16),
                        pltpu.VMEM((S, LANES), F32), pltpu.VMEM((S, 2 * LANES), F32)]
                       + [pltpu.VMEM((t, t), F32)] * FLASH_BUFS
                       + [pltpu.VMEM((t, t), BF16)] * FLASH_BUFS
                       + [pltpu.VMEM((t, LANES), F32)] * FLASH_BUFS
                       + [pltpu.VMEM((t, LANES), F32)] * FLASH_BUFS,
        compiler_params=_params("parallel", "arbitrary"),
        name=kind + "_attn",
    )(table, qa, qb, ka, kb, v)


def _swa_kernel(sink_ref, q_ref, kc_ref, kp_ref, vc_ref, vp_ref, o_ref):
    i = pl.program_id(1)
    t = q_ref.shape[0]
    lane = lax.broadcasted_iota(jnp.int32, (1, LANES), 1)
    pair_lo = (lane % (2 * HALF)) < HALF
    k_all = jnp.concatenate([kp_ref[...], kc_ref[...]], axis=0)
    v_all = jnp.concatenate([vp_ref[...], vc_ref[...]], axis=0)
    row = lax.broadcasted_iota(jnp.int32, (t, WINDOW + t), 0)
    col = lax.broadcasted_iota(jnp.int32, (t, WINDOW + t), 1)
    valid = (col <= row + WINDOW) & (col > row) & ((col >= WINDOW) | (i > 0))
    group = SWA_HEADS // SWA_KV_HEADS
    zero = jnp.zeros((), BF16)
    kv_of = lambda hh: hh // group
    scores = []
    for hh in range(SWA_HEADS):
        kg = k_all[:, LANES * kv_of(hh):LANES * (kv_of(hh) + 1)]
        ke = jnp.where(pair_lo, kg, zero) if hh % 2 == 0 else jnp.where(pair_lo, zero, kg)
        scores.append(_qk(q_ref[:, LANES * (hh // 2):LANES * (hh // 2 + 1)], ke))
    probs, denoms = [], []
    for hh in range(SWA_HEADS):
        sink = sink_ref[hh] * LOG2E
        s = jnp.where(valid, scores[hh], NEG)
        m = jnp.maximum(jnp.max(s, axis=-1, keepdims=True), sink)
        p = jnp.exp2(s - m)
        denoms.append(jnp.sum(p, axis=-1, keepdims=True) + jnp.exp2(sink - m))
        probs.append(p.astype(BF16))
    outs = [jnp.dot(probs[hh], v_all[:, LANES * kv_of(hh):LANES * (kv_of(hh) + 1)], preferred_element_type=F32) / denoms[hh]
            for hh in range(SWA_HEADS)]
    for c in range(SWA_HEADS // 2):
        o_ref[:, LANES * c:LANES * (c + 1)] = jnp.where(lane < SWA_DIM, outs[2 * c], outs[2 * c + 1]).astype(o_ref.dtype)


def _swa_attention(sinks, q, k, v, *, t=256):
    B, S, _ = q.shape
    assert S % t == 0 and t % WINDOW == 0
    r = t // WINDOW
    kv_w = LANES * SWA_KV_HEADS
    cur = pl.BlockSpec((None, t, kv_w), lambda b, i: (b, i, 0))
    prev = pl.BlockSpec((None, WINDOW, kv_w), lambda b, i: (b, jnp.maximum(i * r - 1, 0), 0))
    return pl.pallas_call(
        _swa_kernel,
        grid=(B, S // t),
        in_specs=[
            pl.BlockSpec(memory_space=pltpu.SMEM),
            pl.BlockSpec((None, t, SWA_HEADS * SWA_DIM), lambda b, i: (b, i, 0)),
            cur, prev, cur, prev,
        ],
        out_specs=pl.BlockSpec((None, t, SWA_HEADS * SWA_DIM), lambda b, i: (b, i, 0)),
        out_shape=jax.ShapeDtypeStruct((B, S, SWA_HEADS * SWA_DIM), BF16),
        compiler_params=_params("parallel", "arbitrary"),
        name="swa_attn",
    )(sinks, q, k, k, v, v)


def _outproj_kernel(x_ref, om_ref, os_ref, of_ref, w_ref, o_ref):
    a = om_ref.shape[0] * LANES
    b = a + os_ref.shape[1]
    om = jnp.concatenate([om_ref[h] for h in range(om_ref.shape[0])], axis=1)
    y = jnp.dot(om, w_ref[:a, :], preferred_element_type=F32)
    y += jnp.dot(os_ref[...], w_ref[a:b, :], preferred_element_type=F32)
    of = jnp.concatenate([of_ref[2 * c] + of_ref[2 * c + 1] for c in range(of_ref.shape[0] // 2)], axis=1)
    y += jnp.dot(of, w_ref[b:, :], preferred_element_type=F32)
    o_ref[...] = x_ref[...] + y


def _outproj(x, om, osw, of, w_out, layer, *, tm=512):
    T, D = x.shape
    S = om.shape[2]
    assert S % tm == 0
    sps = S // tm

    def row(width):
        return pl.BlockSpec((tm, width), lambda i: (i, 0))

    def heads(a):
        return pl.BlockSpec((None, a.shape[1], tm, LANES), lambda i: (i // sps, 0, i % sps, 0))

    return pl.pallas_call(
        _outproj_kernel,
        grid=(T // tm,),
        in_specs=[row(D), heads(om), row(osw.shape[1]), heads(of),
                  _resident((None,) + w_out.shape[1:], lambda i: (layer, 0, 0))],
        out_specs=row(D),
        out_shape=jax.ShapeDtypeStruct((T, D), F32),
        compiler_params=_params("parallel"),
        name="outproj",
    )(x, om, osw, of, w_out)


def kernel(x, positions, ffn1_norm, ffn1_w_gate, ffn1_w_up, ffn1_w_down, mix_norm, w_in, mla_q_norm, mla_w_q_b, mla_kv_norm, mla_w_kv_b, swa_sinks, fox_forget_bias, w_out, ffn2_norm, ffn2_w_gate, ffn2_w_up, ffn2_w_down, final_norm):
    B, S, D = x.shape
    depth = w_in.shape[0]
    T = B * S

    bf = lambda w: w.astype(BF16)
    w_in_r = _take_cols(w_in, _IN_PERM)
    wqb_r = _take_cols(mla_w_q_b, _QB_PERM)
    wkvb_r = _take_cols(mla_w_kv_b, _KVB_PERM)
    ffn_w = [(bf(ffn1_w_gate), bf(ffn1_w_up), bf(ffn1_w_down)), (bf(ffn2_w_gate), bf(ffn2_w_up), bf(ffn2_w_down))]
    ffn_g = [ffn1_norm[:, None, :], ffn2_norm[:, None, :]]
    w_out_b = bf(w_out)
    mix_g = mix_norm[:, None, :]
    qn = mla_q_norm[:, None, :]
    kvn = mla_kv_norm[:, None, :]
    fb = jnp.pad(jnp.tile(fox_forget_bias, (1, GATE_COPIES)), ((0, 0), (0, LANES - GATE_COPIES * FOX_HEADS)))[:, None, :]
    fin_g = final_norm[None, :]

    inv_freq = ROPE_THETA ** (-jnp.arange(0, 2 * HALF, 2, dtype=F32) / (2 * HALF))
    ang = positions.astype(F32).reshape(T, 1) * jnp.tile(inv_freq, LANES // HALF)
    sign = jnp.where(jnp.arange(LANES) < LANES // 2, -1.0, 1.0).astype(F32)
    cos_t = jnp.cos(ang)
    sin_t = jnp.sin(ang) * sign

    xt = x.reshape(T, D)
    sh = lambda a: a.reshape(B, S, a.shape[-1])
    for l in range(depth):
        xt = _ffn(xt, ffn_g[0], *ffn_w[0], l, fin_g, final_norm=False)
        qmn, qmp, kmn, kmr, vm, qs, ks, vs, qf, kf, vf, qaug, kaug = _mixin(
            xt, mix_g, w_in_r, qn, wqb_r, kvn, wkvb_r, cos_t, sin_t, fb, l, batch=B)
        o_mla = _flash_attention(qmn, qmp, kmn, sh(kmr), vm, kind="mla")
        o_swa = _swa_attention(swa_sinks[l], sh(qs), sh(ks), sh(vs))
        o_fox = _flash_attention(qf, sh(qaug), kf, sh(kaug), vf, kind="fox")
        xt = _outproj(xt, o_mla, o_swa.reshape(T, -1), o_fox, w_out_b, l)
        xt = _ffn(xt, ffn_g[1], *ffn_w[1], l, fin_g, final_norm=(l == depth - 1))
    return xt.reshape(B, S, D)
```

```python
import functools
import math

import numpy as np
import jax
import jax.numpy as jnp
from jax import lax
from jax.experimental import pallas as pl
from jax.experimental.pallas import tpu as pltpu

RMS_EPS = 1e-6
ROPE_THETA = 10000.0

MLA_HEADS = 8
MLA_Q_LORA = 512
MLA_KV_LORA = 256
MLA_NOPE = 128
MLA_ROPE = 64
MLA_V = 128

SWA_HEADS = 8
SWA_KV_HEADS = 2
SWA_DIM = 64
WINDOW = 128

FOX_HEADS = 8
FOX_DIM = 64

LANES = 128
HALF = 32
VMEM_LIMIT = 60 * 1024 * 1024
FFN_ROW_CHUNK = 256
FLASH_LAG = 2
FLASH_BUFS = 2 * FLASH_LAG
FLASH_UNROLL = 7
NEG = -1e30
LOG2E = math.log2(math.e)
GATE_COPIES = 6

F32 = jnp.float32
BF16 = jnp.bfloat16

_O_CQ, _O_CKV, _O_QF, _O_KF, _O_VF, MAIN_COLS = 0, 512, 768, 1280, 1792, 2304
_O_KR, _O_QS, _O_KS, _O_VS, _O_FG, SPECIAL_COLS = 0, 128, 640, 896, 1152, 1280


def _chunk_pair(base_a, base_b):
    r = np.arange(HALF)
    return np.concatenate([base_a + r, base_b + r, base_a + HALF + r, base_b + HALF + r])


def _in_perm():
    src = [0, 512, 768, 832, 1344, 1472, 1600, 2112, 2624, 3136]
    zero = 3144
    main = np.concatenate([np.arange(src[0], src[2]), np.arange(src[6], src[9])]).astype(np.int32)
    cols = [_chunk_pair(src[2], src[2])]
    for c in range(SWA_HEADS // 2):
        cols.append(_chunk_pair(src[3] + SWA_DIM * 2 * c, src[3] + SWA_DIM * (2 * c + 1)))
    for g in range(SWA_KV_HEADS):
        cols.append(_chunk_pair(src[4] + SWA_DIM * g, src[4] + SWA_DIM * g))
    for g in range(SWA_KV_HEADS):
        v = src[5] + SWA_DIM * g + np.arange(SWA_DIM)
        cols.append(np.concatenate([v, v]))
    gate = np.tile(np.arange(src[9], src[9] + FOX_HEADS), GATE_COPIES)
    cols.append(np.concatenate([gate, np.full(LANES - gate.shape[0], zero)]))
    special = np.concatenate(cols).astype(np.int32)
    assert main.shape[0] == MAIN_COLS and special.shape[0] == SPECIAL_COLS
    return main, special


def _qb_perm():
    per = MLA_NOPE + MLA_ROPE
    cols = [per * h + np.arange(MLA_NOPE) for h in range(MLA_HEADS)]
    for c in range(MLA_HEADS // 2):
        cols.append(_chunk_pair(per * 2 * c + MLA_NOPE, per * (2 * c + 1) + MLA_NOPE))
    return np.concatenate(cols).astype(np.int32)


def _kvb_perm():
    per = MLA_NOPE + MLA_V
    k = [per * h + np.arange(MLA_NOPE) for h in range(MLA_HEADS)]
    v = [per * h + MLA_NOPE + np.arange(MLA_V) for h in range(MLA_HEADS)]
    return np.concatenate(k + v).astype(np.int32)


(_IN_MAIN_PERM, _IN_SPECIAL_PERM), _QB_PERM, _KVB_PERM = _in_perm(), _qb_perm(), _kvb_perm()


def _take_cols(w, perm):
    ncol = w.shape[-1]
    same_run = lambda k: (perm[k] == ncol) if perm[k - 1] == ncol else (perm[k] == perm[k - 1] + 1 and perm[k] != ncol)
    cuts = [0] + [k for k in range(1, len(perm)) if not same_run(k)] + [len(perm)]
    parts = []
    for a, b in zip(cuts[:-1], cuts[1:]):
        if perm[a] == ncol:
            parts.append(jnp.zeros(w.shape[:-1] + (b - a,), w.dtype))
        else:
            parts.append(w[..., int(perm[a]):int(perm[a]) + (b - a)])
    return jnp.concatenate(parts, axis=-1).astype(BF16)


def _params(*sem):
    return pltpu.CompilerParams(dimension_semantics=sem, vmem_limit_bytes=VMEM_LIMIT)


def _resident(block_shape, index_map):
    return pl.BlockSpec(block_shape, index_map, pipeline_mode=pl.Buffered(1))


def _rms(x, gain):
    ms = jnp.mean(x * x, axis=-1, keepdims=True)
    return x * lax.rsqrt(ms + RMS_EPS) * gain


def _ffn_kernel(x_ref, g_ref, wg_ref, wu_ref, wd_ref, fg_ref, o_ref, h_sc, *, final_norm):
    f = pl.program_id(1)

    def row_chunks(body):
        def step(r, carry):
            body(pl.ds(pl.multiple_of(r * FFN_ROW_CHUNK, FFN_ROW_CHUNK), FFN_ROW_CHUNK))
            return carry
        lax.fori_loop(0, x_ref.shape[0] // FFN_ROW_CHUNK, step, 0)

    @pl.when(f == 0)
    def _():
        def prologue(rows):
            h_sc[rows, :] = _rms(x_ref[rows, :], g_ref[...]).astype(BF16)
            o_ref[rows, :] = jnp.zeros((FFN_ROW_CHUNK, o_ref.shape[1]), F32)
        row_chunks(prologue)

    h = h_sc[...]
    gate = jnp.dot(h, wg_ref[...], preferred_element_type=F32)
    up = jnp.dot(h, wu_ref[...], preferred_element_type=F32)
    act = (gate * (1.0 / (1.0 + jnp.exp(-gate))) * up).astype(BF16)
    o_ref[...] += jnp.dot(act, wd_ref[...], preferred_element_type=F32)

    @pl.when(f == pl.num_programs(1) - 1)
    def _():
        def epilogue(rows):
            y = x_ref[rows, :] + 0.5 * o_ref[rows, :]
            if final_norm:
                y = _rms(y, fg_ref[...])
            o_ref[rows, :] = y
        row_chunks(epilogue)


def _ffn(x, gain, wg, wu, wd, layer, final_gain, *, final_norm, tm=1024, tf=512):
    T, D = x.shape
    F = wg.shape[-1]
    assert T % tm == 0 and F % tf == 0
    return pl.pallas_call(
        functools.partial(_ffn_kernel, final_norm=final_norm),
        grid=(T // tm, F // tf),
        in_specs=[
            pl.BlockSpec((tm, D), lambda i, f: (i, 0)),
            pl.BlockSpec((None, 1, D), lambda i, f: (layer, 0, 0)),
            pl.BlockSpec((None, D, tf), lambda i, f: (layer, 0, f)),
            pl.BlockSpec((None, D, tf), lambda i, f: (layer, 0, f)),
            pl.BlockSpec((None, tf, D), lambda i, f: (layer, f, 0)),
            pl.BlockSpec((1, D), lambda i, f: (0, 0)),
        ],
        out_specs=pl.BlockSpec((tm, D), lambda i, f: (i, 0)),
        out_shape=jax.ShapeDtypeStruct((T, D), F32),
        scratch_shapes=[pltpu.VMEM((tm, D), BF16)],
        compiler_params=_params("parallel", "arbitrary"),
        name="ffn",
    )(x, gain, wg, wu, wd, final_gain)


def _rope(x, cos, sin_signed):
    return x * cos + pltpu.roll(x, 2 * HALF, axis=1) * sin_signed


def _split3(x):
    a1 = x.astype(BF16)
    r1 = x - a1.astype(F32)
    a2 = r1.astype(BF16)
    a3 = (r1 - a2.astype(F32)).astype(BF16)
    return a1, a2, a3


def _store_chunks(ref, value):
    for c in range(ref.shape[0]):
        ref[c] = value[:, LANES * c:LANES * (c + 1)]


def _mixin_kernel(x_ref, g_ref, wmain_ref, wspecial_ref, qn_ref, wqb_ref, kvn_ref, wkvb_ref, cos_ref, sin_ref, fb_ref,
                  qmn_ref, qmp_ref, kmn_ref, kmr_ref, vm_ref, qs_ref, ks_ref, vs_ref,
                  qf_ref, kf_ref, vf_ref, qaug_ref, kaug_ref, carry_sc, *, steps_per_seq):
    i = pl.program_id(0)
    tm = x_ref.shape[0]

    @pl.when(i % steps_per_seq == 0)
    def _():
        carry_sc[...] = jnp.zeros_like(carry_sc)

    h = _rms(x_ref[...], g_ref[...]).astype(BF16)
    cos = cos_ref[...]
    sin = sin_ref[...]
    lane = lax.broadcasted_iota(jnp.int32, (1, LANES), 1)
    n_nope = MLA_HEADS * MLA_NOPE

    def proj(a, b):
        return jnp.dot(h, wmain_ref[:, a:b], preferred_element_type=F32)

    def proj_special(a, b):
        return jnp.dot(h, wspecial_ref[:, a:b], preferred_element_type=F32)

    cq = _rms(proj(_O_CQ, _O_CKV), qn_ref[...]).astype(BF16)
    ckv = _rms(proj(_O_CKV, _O_QF), kvn_ref[...]).astype(BF16)
    fl = proj_special(_O_FG, SPECIAL_COLS) + fb_ref[...]
    logf = jnp.minimum(fl, 0.0) - jnp.log(1.0 + jnp.exp(-jnp.abs(fl)))

    kmr_ref[...] = _rope(proj_special(_O_KR, _O_QS), cos, sin).astype(BF16)
    qs = proj_special(_O_QS, _O_KS)
    for c in range(SWA_HEADS // 2):
        sl = slice(LANES * c, LANES * (c + 1))
        qs_ref[:, sl] = (_rope(qs[:, sl], cos, sin) * (SWA_DIM ** -0.5 * LOG2E)).astype(BF16)
    ks = proj_special(_O_KS, _O_VS)
    for g in range(SWA_KV_HEADS):
        sl = slice(LANES * g, LANES * (g + 1))
        ks_ref[:, sl] = _rope(ks[:, sl], cos, sin).astype(BF16)
    vs_ref[...] = proj_special(_O_VS, _O_FG).astype(BF16)

    _store_chunks(qf_ref, (proj(_O_QF, _O_KF) * (FOX_DIM ** -0.5 * LOG2E)).astype(BF16))
    _store_chunks(kf_ref, proj(_O_KF, _O_VF).astype(BF16))
    _store_chunks(vf_ref, proj(_O_VF, MAIN_COLS).astype(BF16))

    row = lax.broadcasted_iota(jnp.int32, (tm, tm), 0)
    col = lax.broadcasted_iota(jnp.int32, (tm, tm), 1)
    tri = (col <= row).astype(BF16)
    csum = carry_sc[...]
    for term in _split3(logf):
        csum = csum + jnp.dot(tri, term, preferred_element_type=F32)
    carry_sc[...] = csum[tm - 1:tm, :]

    c1, c2, c3 = (term.astype(F32) for term in _split3(csum * LOG2E))
    g8 = lane // FOX_HEADS
    ones = jnp.where(g8 < GATE_COPIES, 1.0, 0.0)
    qaug_ref[...] = jnp.where(g8 < 3, ones, jnp.where(g8 == 3, c1, jnp.where(g8 == 4, c2, jnp.where(g8 == 5, c3, 0.0)))).astype(BF16)
    kaug_ref[...] = jnp.where(g8 == 0, -c1, jnp.where(g8 == 1, -c2, jnp.where(g8 == 2, -c3, ones))).astype(BF16)

    q = jnp.dot(cq, wqb_ref[...], preferred_element_type=F32)
    q_scale = (MLA_NOPE + MLA_ROPE) ** -0.5 * LOG2E
    _store_chunks(qmn_ref, (q[:, :n_nope] * q_scale).astype(BF16))
    for c in range(MLA_HEADS // 2):
        qmp_ref[c] = (_rope(q[:, n_nope + LANES * c:n_nope + LANES * (c + 1)], cos, sin) * q_scale).astype(BF16)
    kv = jnp.dot(ckv, wkvb_ref[...], preferred_element_type=F32)
    _store_chunks(kmn_ref, kv[:, :n_nope].astype(BF16))
    _store_chunks(vm_ref, kv[:, n_nope:].astype(BF16))


def _mixin(x, gain, w_main, w_special, qn, wqb, kvn, wkvb, cos, sin, fb, layer, *, batch, tm=256):
    T, D = x.shape
    S = T // batch
    assert S % tm == 0
    sps = S // tm

    def row(width):
        return pl.BlockSpec((tm, width), lambda i: (i, 0))

    def wres(a):
        return _resident((None,) + a.shape[1:], lambda i: (layer, 0, 0))

    def chunked(chunks):
        return (jax.ShapeDtypeStruct((batch, chunks, S, LANES), BF16),
                pl.BlockSpec((None, chunks, tm, LANES), lambda i: (i // sps, 0, i % sps, 0)))

    def flat(width):
        return jax.ShapeDtypeStruct((T, width), BF16), row(width)

    outs = dict(qmn=chunked(MLA_HEADS), qmp=chunked(MLA_HEADS // 2), kmn=chunked(MLA_HEADS), kmr=flat(LANES),
                vm=chunked(MLA_HEADS), qs=flat(SWA_DIM * SWA_HEADS), ks=flat(LANES * SWA_KV_HEADS),
                vs=flat(LANES * SWA_KV_HEADS), qf=chunked(FOX_HEADS // 2), kf=chunked(FOX_HEADS // 2),
                vf=chunked(FOX_HEADS // 2), qaug=flat(LANES), kaug=flat(LANES))
    return pl.pallas_call(
        functools.partial(_mixin_kernel, steps_per_seq=sps),
        grid=(T // tm,),
        in_specs=[row(D), wres(gain), wres(w_main), wres(w_special), wres(qn), wres(wqb), wres(kvn), wres(wkvb),
                  row(LANES), row(LANES), wres(fb)],
        out_specs=[spec for _, spec in outs.values()],
        out_shape=[shape for shape, _ in outs.values()],
        scratch_shapes=[pltpu.VMEM((1, LANES), F32)],
        compiler_params=_params("arbitrary"),
        name="mixin",
    )(x, gain, w_main, w_special, qn, wqb, kvn, wkvb, cos, sin, fb)


def _qk(q, k):
    return lax.dot_general(q, k, (((1,), (1,)), ((), ())), preferred_element_type=F32)


def _causal_mask(t):
    row = lax.broadcasted_iota(jnp.int32, (t, t), 0)
    col = lax.broadcasted_iota(jnp.int32, (t, t), 1)
    return col <= row


def _flash_kernel(tab_ref, qa_ref, qb_ref, ka_ref, kb_ref, v_ref, o_ref,
                  q_sc, k_sc, v_sc, m_sc, acc_sc, *bufs, kind, t, n_off):
    h = pl.program_id(1)
    n = q_sc.shape[0] // t
    lane = lax.broadcasted_iota(jnp.int32, (1, LANES), 1)
    even = (h % 2) == 0
    zero = jnp.zeros((), BF16)

    if kind == "mla":
        mine = ((lane % (2 * HALF)) < HALF) == even
        q_sc[:, :LANES] = qa_ref[...]
        q_sc[:, LANES:] = qb_ref[...]
        k_sc[:, :LANES] = ka_ref[...]
        k_sc[:, LANES:] = jnp.where(mine, kb_ref[...], zero)
        v_sc[:, :LANES] = v_ref[...]
    else:
        mine = (lane < FOX_DIM) == even
        gate = ((lane % FOX_HEADS) == h) & (lane < GATE_COPIES * FOX_HEADS)
        q_sc[:, :LANES] = qa_ref[...]
        q_sc[:, LANES:] = jnp.where(gate, qb_ref[...], zero)
        k_sc[:, :LANES] = jnp.where(mine, ka_ref[...], zero)
        k_sc[:, LANES:] = kb_ref[...]
        v_sc[:, :LANES] = jnp.where(mine, v_ref[...], zero)
    v_sc[:, LANES:] = jnp.ones((v_sc.shape[0], LANES), BF16)

    def rows(j):
        return pl.ds(pl.multiple_of(j * t, t), t)

    s_bufs, p_bufs, a_bufs, r_bufs = (bufs[FLASH_BUFS * g:FLASH_BUFS * (g + 1)] for g in range(4))

    def scores(i, j, slot):
        s = _qk(q_sc[rows(i), :], k_sc[rows(j), :])
        s_bufs[slot][...] = s
        r_bufs[slot][...] = functools.reduce(jnp.maximum, [s[:, LANES * c:LANES * (c + 1)] for c in range(t // LANES)])

    def softmax(par, i, first):
        s = s_bufs[par][...]
        if first:
            s = jnp.where(_causal_mask(t), s, NEG)
        tiles = [s[:, LANES * c:LANES * (c + 1)] for c in range(t // LANES)]
        tile_max = functools.reduce(jnp.maximum, tiles) if first else r_bufs[par][...]
        m_new = jnp.broadcast_to(jnp.max(tile_max, axis=-1, keepdims=True), (t, LANES))
        if not first:
            m_old = m_sc[rows(i), :]
            m_new = jnp.maximum(m_old, m_new)
            a_bufs[par][...] = jnp.exp2(m_old - m_new)
        for c, tile in enumerate(tiles):
            p_bufs[par][:, LANES * c:LANES * (c + 1)] = jnp.exp2(tile - m_new).astype(BF16)
        m_sc[rows(i), :] = m_new

    def accumulate(par, i, j, first):
        pv = jnp.dot(p_bufs[par][...], v_sc[rows(j), :], preferred_element_type=F32)
        if first:
            acc_sc[rows(i), :] = pv
        else:
            a = a_bufs[par][...]
            acc_sc[rows(i), :] = jnp.concatenate([a, a], axis=1) * acc_sc[rows(i), :] + pv

    def pipeline(base, count, first, lag, unroll=1):
        nb = 2 * lag
        blk = lambda a: (tab_ref[0, base + a], tab_ref[1, base + a])

        def step(a, u, with_acc=True):
            if with_acc:
                accumulate((u - lag) % nb, *blk(a - lag), first)
            scores(*blk(a + lag), (u + lag) % nb)
            softmax(u, blk(a)[0], first)

        for a in range(lag):
            scores(*blk(a), a)
        for a in range(min(lag, count)):
            step(a, a, with_acc=False)
        main = max(count - lag, 0)

        span = nb * unroll

        def body(k, carry):
            for u in range(span):
                step(lag + span * k + u, (lag + u) % nb)
            return carry

        lax.fori_loop(0, main // span, body, 0)
        for a in range(lag + main // span * span, count):
            step(a, a % nb)
        for a in range(main, count):
            accumulate(a % nb, *blk(a), first)

    pipeline(0, n, True, 1)
    pipeline(n, n_off, False, FLASH_LAG, unroll=FLASH_UNROLL)

    def finish(i, carry):
        o_ref[rows(i), :] = (acc_sc[rows(i), :LANES] / acc_sc[rows(i), LANES:]).astype(o_ref.dtype)
        return carry

    lax.fori_loop(0, n, finish, 0)


def _flash_attention(qa, qb, ka, kb, v, *, kind, t=512):
    B, _, S, _ = qa.shape
    heads = MLA_HEADS if kind == "mla" else FOX_HEADS
    n = S // t
    assert S % t == 0 and n % 2 == 0
    diag = [(i, i) for i in range(n)]
    off = [(i, j) for j in range(n - 1) for i in range(j + 1, n)]
    table = jnp.asarray(np.array(diag + off + off[-1:] * (2 * FLASH_LAG), np.int32).T)
    shared = lambda **kw: pl.BlockSpec((None, S, LANES), lambda b, h: (b, 0, 0), **kw)
    own = lambda **kw: pl.BlockSpec((None, None, S, LANES), lambda b, h: (b, h, 0, 0), **kw)
    pair = lambda **kw: pl.BlockSpec((None, None, S, LANES), lambda b, h: (b, h // 2, 0, 0), **kw)
    single = dict(pipeline_mode=pl.Buffered(1))
    if kind == "mla":
        in_specs = [own(**single), pair(**single), own(), shared(), own()]
    else:
        in_specs = [pair(**single), shared(**single), pair(), shared(), pair()]
    return pl.pallas_call(
        functools.partial(_flash_kernel, kind=kind, t=t, n_off=len(off)),
        grid=(B, heads),
        in_specs=[pl.BlockSpec(memory_space=pltpu.SMEM)] + in_specs,
        out_specs=own(),
        out_shape=jax.ShapeDtypeStruct((B, heads, S, LANES), BF16),
        scratch_shapes=[pltpu.VMEM((S, 2 * LANES), BF16), pltpu.VMEM((S, 2 * LANES), BF16),
                        pltpu.VMEM((S, 2 * LANES), BF16),
                        pltpu.VMEM((S, LANES), F32), pltpu.VMEM((S, 2 * LANES), F32)]
                       + [pltpu.VMEM((t, t), F32)] * FLASH_BUFS
                       + [pltpu.VMEM((t, t), BF16)] * FLASH_BUFS
                       + [pltpu.VMEM((t, LANES), F32)] * FLASH_BUFS
                       + [pltpu.VMEM((t, LANES), F32)] * FLASH_BUFS,
        compiler_params=_params("parallel", "arbitrary"),
        name=kind + "_attn",
    )(table, qa, qb, ka, kb, v)


def _swa_kernel(sink_ref, q_ref, kc_ref, kp_ref, vc_ref, vp_ref, o_ref):
    i = pl.program_id(1)
    t = q_ref.shape[0]
    lane = lax.broadcasted_iota(jnp.int32, (1, LANES), 1)
    pair_lo = (lane % (2 * HALF)) < HALF
    k_all = jnp.concatenate([kp_ref[...], kc_ref[...]], axis=0)
    v_all = jnp.concatenate([vp_ref[...], vc_ref[...]], axis=0)
    row = lax.broadcasted_iota(jnp.int32, (t, WINDOW + t), 0)
    col = lax.broadcasted_iota(jnp.int32, (t, WINDOW + t), 1)
    valid = (col <= row + WINDOW) & (col > row) & ((col >= WINDOW) | (i > 0))
    group = SWA_HEADS // SWA_KV_HEADS
    zero = jnp.zeros((), BF16)
    kv_of = lambda hh: hh // group
    scores = []
    for hh in range(SWA_HEADS):
        kg = k_all[:, LANES * kv_of(hh):LANES * (kv_of(hh) + 1)]
        ke = jnp.where(pair_lo, kg, zero) if hh % 2 == 0 else jnp.where(pair_lo, zero, kg)
        scores.append(_qk(q_ref[:, LANES * (hh // 2):LANES * (hh // 2 + 1)], ke))
    probs, denoms = [], []
    for hh in range(SWA_HEADS):
        sink = sink_ref[hh] * LOG2E
        s = jnp.where(valid, scores[hh], NEG)
        m = jnp.maximum(jnp.max(s, axis=-1, keepdims=True), sink)
        p = jnp.exp2(s - m)
        denoms.append(jnp.sum(p, axis=-1, keepdims=True) + jnp.exp2(sink - m))
        probs.append(p.astype(BF16))
    outs = [jnp.dot(probs[hh], v_all[:, LANES * kv_of(hh):LANES * (kv_of(hh) + 1)], preferred_element_type=F32) / denoms[hh]
            for hh in range(SWA_HEADS)]
    for c in range(SWA_HEADS // 2):
        o_ref[:, LANES * c:LANES * (c + 1)] = jnp.where(lane < SWA_DIM, outs[2 * c], outs[2 * c + 1]).astype(o_ref.dtype)


def _swa_attention(sinks, q, k, v, *, t=256):
    B, S, _ = q.shape
    assert S % t == 0 and t % WINDOW == 0
    r = t // WINDOW
    kv_w = LANES * SWA_KV_HEADS
    cur = pl.BlockSpec((None, t, kv_w), lambda b, i: (b, i, 0))
    prev = pl.BlockSpec((None, WINDOW, kv_w), lambda b, i: (b, jnp.maximum(i * r - 1, 0), 0))
    return pl.pallas_call(
        _swa_kernel,
        grid=(B, S // t),
        in_specs=[
            pl.BlockSpec(memory_space=pltpu.SMEM),
            pl.BlockSpec((None, t, SWA_HEADS * SWA_DIM), lambda b, i: (b, i, 0)),
            cur, prev, cur, prev,
        ],
        out_specs=pl.BlockSpec((None, t, SWA_HEADS * SWA_DIM), lambda b, i: (b, i, 0)),
        out_shape=jax.ShapeDtypeStruct((B, S, SWA_HEADS * SWA_DIM), BF16),
        compiler_params=_params("parallel", "arbitrary"),
        name="swa_attn",
    )(sinks, q, k, k, v, v)


def _outproj_kernel(x_ref, om_ref, os_ref, of_ref, w_ref, o_ref):
    a = om_ref.shape[0] * LANES
    b = a + os_ref.shape[1]
    om = jnp.concatenate([om_ref[h] for h in range(om_ref.shape[0])], axis=1)
    y = jnp.dot(om, w_ref[:a, :], preferred_element_type=F32)
    y += jnp.dot(os_ref[...], w_ref[a:b, :], preferred_element_type=F32)
    of = jnp.concatenate([of_ref[2 * c] + of_ref[2 * c + 1] for c in range(of_ref.shape[0] // 2)], axis=1)
    y += jnp.dot(of, w_ref[b:, :], preferred_element_type=F32)
    o_ref[...] = x_ref[...] + y


def _outproj(x, om, osw, of, w_out, layer, *, tm=512):
    T, D = x.shape
    S = om.shape[2]
    assert S % tm == 0
    sps = S // tm

    def row(width):
        return pl.BlockSpec((tm, width), lambda i: (i, 0))

    def heads(a):
        return pl.BlockSpec((None, a.shape[1], tm, LANES), lambda i: (i // sps, 0, i % sps, 0))

    return pl.pallas_call(
        _outproj_kernel,
        grid=(T // tm,),
        in_specs=[row(D), heads(om), row(osw.shape[1]), heads(of),
                  _resident((None,) + w_out.shape[1:], lambda i: (layer, 0, 0))],
        out_specs=row(D),
        out_shape=jax.ShapeDtypeStruct((T, D), F32),
        compiler_params=_params("parallel"),
        name="outproj",
    )(x, om, osw, of, w_out)


def kernel(x, positions, ffn1_norm, ffn1_w_gate, ffn1_w_up, ffn1_w_down, mix_norm, w_in, mla_q_norm, mla_w_q_b, mla_kv_norm, mla_w_kv_b, swa_sinks, fox_forget_bias, w_out, ffn2_norm, ffn2_w_gate, ffn2_w_up, ffn2_w_down, final_norm):
    B, S, D = x.shape
    depth = w_in.shape[0]
    T = B * S

    bf = lambda w: w.astype(BF16)
    w_main = _take_cols(w_in, _IN_MAIN_PERM)
    w_special = _take_cols(w_in, _IN_SPECIAL_PERM)
    wqb_r = _take_cols(mla_w_q_b, _QB_PERM)
    wkvb_r = _take_cols(mla_w_kv_b, _KVB_PERM)
    ffn_w = [(bf(ffn1_w_gate), bf(ffn1_w_up), bf(ffn1_w_down)), (bf(ffn2_w_gate), bf(ffn2_w_up), bf(ffn2_w_down))]
    ffn_g = [ffn1_norm[:, None, :], ffn2_norm[:, None, :]]
    w_out_b = bf(w_out)
    mix_g = mix_norm[:, None, :]
    qn = mla_q_norm[:, None, :]
    kvn = mla_kv_norm[:, None, :]
    fb = jnp.pad(jnp.tile(fox_forget_bias, (1, GATE_COPIES)), ((0, 0), (0, LANES - GATE_COPIES * FOX_HEADS)))[:, None, :]
    fin_g = final_norm[None, :]

    inv_freq = ROPE_THETA ** (-jnp.arange(0, 2 * HALF, 2, dtype=F32) / (2 * HALF))
    ang = positions.astype(F32).reshape(T, 1) * jnp.tile(inv_freq, LANES // HALF)
    sign = jnp.where(jnp.arange(LANES) < LANES // 2, -1.0, 1.0).astype(F32)
    cos_t = jnp.cos(ang)
    sin_t = jnp.sin(ang) * sign

    xt = x.reshape(T, D)
    sh = lambda a: a.reshape(B, S, a.shape[-1])
    for l in range(depth):
        xt = _ffn(xt, ffn_g[0], *ffn_w[0], l, fin_g, final_norm=False)
        qmn, qmp, kmn, kmr, vm, qs, ks, vs, qf, kf, vf, qaug, kaug = _mixin(
            xt, mix_g, w_main, w_special, qn, wqb_r, kvn, wkvb_r, cos_t, sin_t, fb, l, batch=B)
        o_mla = _flash_attention(qmn, qmp, kmn, sh(kmr), vm, kind="mla")
        o_swa = _swa_attention(swa_sinks[l], sh(qs), sh(ks), sh(vs))
        o_fox = _flash_attention(qf, sh(qaug), kf, sh(kaug), vf, kind="fox")
        xt = _outproj(xt, o_mla, o_swa.reshape(T, -1), o_fox, w_out_b, l)
        xt = _ffn(xt, ffn_g[1], *ffn_w[1], l, fin_g, final_norm=(l == depth - 1))
    return xt.reshape(B, S, D)
```

```python
import functools
import math

import numpy as np
import jax
import jax.numpy as jnp
from jax import lax
from jax.experimental import pallas as pl
from jax.experimental.pallas import tpu as pltpu

RMS_EPS = 1e-6
ROPE_THETA = 10000.0

MLA_HEADS = 8
MLA_Q_LORA = 512
MLA_KV_LORA = 256
MLA_NOPE = 128
MLA_ROPE = 64
MLA_V = 128

SWA_HEADS = 8
SWA_KV_HEADS = 2
SWA_DIM = 64
WINDOW = 128

FOX_HEADS = 8
FOX_DIM = 64

LANES = 128
HALF = 32
VMEM_LIMIT = 60 * 1024 * 1024
FFN_ROW_CHUNK = 256
FLASH_LAG = 2
FLASH_BUFS = 2 * FLASH_LAG
FLASH_UNROLL = 7
NEG = -1e30
LOG2E = math.log2(math.e)
GATE_COPIES = 6

F32 = jnp.float32
BF16 = jnp.bfloat16

_O_CQ, _O_CKV, _O_KR, _O_QS, _O_KS, _O_VS, _O_QF, _O_KF, _O_VF, _O_FG, IN_COLS_R = (
    0, 512, 768, 896, 1408, 1664, 1920, 2432, 2944, 3456, 3584)


def _chunk_pair(base_a, base_b):
    r = np.arange(HALF)
    return np.concatenate([base_a + r, base_b + r, base_a + HALF + r, base_b + HALF + r])


def _in_perm():
    src = [0, 512, 768, 832, 1344, 1472, 1600, 2112, 2624, 3136]
    zero = 3144
    cols = [np.arange(src[0], src[0] + 512), np.arange(src[1], src[1] + 256)]
    cols.append(_chunk_pair(src[2], src[2]))
    for c in range(SWA_HEADS // 2):
        cols.append(_chunk_pair(src[3] + SWA_DIM * 2 * c, src[3] + SWA_DIM * (2 * c + 1)))
    for g in range(SWA_KV_HEADS):
        cols.append(_chunk_pair(src[4] + SWA_DIM * g, src[4] + SWA_DIM * g))
    for g in range(SWA_KV_HEADS):
        v = src[5] + SWA_DIM * g + np.arange(SWA_DIM)
        cols.append(np.concatenate([v, v]))
    cols += [np.arange(src[6], src[6] + 512), np.arange(src[7], src[7] + 512), np.arange(src[8], src[8] + 512)]
    gate = np.tile(np.arange(src[9], src[9] + FOX_HEADS), GATE_COPIES)
    cols.append(np.concatenate([gate, np.full(LANES - gate.shape[0], zero)]))
    out = np.concatenate(cols).astype(np.int32)
    assert out.shape[0] == IN_COLS_R
    return out


def _qb_perm():
    per = MLA_NOPE + MLA_ROPE
    cols = [per * h + np.arange(MLA_NOPE) for h in range(MLA_HEADS)]
    for c in range(MLA_HEADS // 2):
        cols.append(_chunk_pair(per * 2 * c + MLA_NOPE, per * (2 * c + 1) + MLA_NOPE))
    return np.concatenate(cols).astype(np.int32)


def _kvb_perm():
    per = MLA_NOPE + MLA_V
    k = [per * h + np.arange(MLA_NOPE) for h in range(MLA_HEADS)]
    v = [per * h + MLA_NOPE + np.arange(MLA_V) for h in range(MLA_HEADS)]
    return np.concatenate(k + v).astype(np.int32)


_IN_PERM, _QB_PERM, _KVB_PERM = _in_perm(), _qb_perm(), _kvb_perm()


def _take_cols(w, perm):
    ncol = w.shape[-1]
    same_run = lambda k: (perm[k] == ncol) if perm[k - 1] == ncol else (perm[k] == perm[k - 1] + 1 and perm[k] != ncol)
    cuts = [0] + [k for k in range(1, len(perm)) if not same_run(k)] + [len(perm)]
    parts = []
    for a, b in zip(cuts[:-1], cuts[1:]):
        if perm[a] == ncol:
            parts.append(jnp.zeros(w.shape[:-1] + (b - a,), w.dtype))
        else:
            parts.append(w[..., int(perm[a]):int(perm[a]) + (b - a)])
    return jnp.concatenate(parts, axis=-1).astype(BF16)


def _params(*sem):
    return pltpu.CompilerParams(dimension_semantics=sem, vmem_limit_bytes=VMEM_LIMIT)


def _resident(block_shape, index_map):
    return pl.BlockSpec(block_shape, index_map, pipeline_mode=pl.Buffered(1))


def _rms(x, gain):
    ms = jnp.mean(x * x, axis=-1, keepdims=True)
    return x * lax.rsqrt(ms + RMS_EPS) * gain


def _ffn_kernel(x_ref, g_ref, wg_ref, wu_ref, wd_ref, fg_ref, o_ref, h_sc, *, final_norm):
    f = pl.program_id(1)

    def row_chunks(body):
        def step(r, carry):
            body(pl.ds(pl.multiple_of(r * FFN_ROW_CHUNK, FFN_ROW_CHUNK), FFN_ROW_CHUNK))
            return carry
        lax.fori_loop(0, x_ref.shape[0] // FFN_ROW_CHUNK, step, 0)

    @pl.when(f == 0)
    def _():
        def prologue(rows):
            h_sc[rows, :] = _rms(x_ref[rows, :], g_ref[...]).astype(BF16)
            o_ref[rows, :] = jnp.zeros((FFN_ROW_CHUNK, o_ref.shape[1]), F32)
        row_chunks(prologue)

    h = h_sc[...]
    gate = jnp.dot(h, wg_ref[...], preferred_element_type=F32)
    up = jnp.dot(h, wu_ref[...], preferred_element_type=F32)
    act = (gate * (1.0 / (1.0 + jnp.exp(-gate))) * up).astype(BF16)
    o_ref[...] += jnp.dot(act, wd_ref[...], preferred_element_type=F32)

    @pl.when(f == pl.num_programs(1) - 1)
    def _():
        def epilogue(rows):
            y = x_ref[rows, :] + 0.5 * o_ref[rows, :]
            if final_norm:
                y = _rms(y, fg_ref[...])
            o_ref[rows, :] = y
        row_chunks(epilogue)


def _ffn(x, gain, wg, wu, wd, layer, final_gain, *, final_norm, tm=1024, tf=512):
    T, D = x.shape
    F = wg.shape[-1]
    assert T % tm == 0 and F % tf == 0
    return pl.pallas_call(
        functools.partial(_ffn_kernel, final_norm=final_norm),
        grid=(T // tm, F // tf),
        in_specs=[
            pl.BlockSpec((tm, D), lambda i, f: (i, 0)),
            pl.BlockSpec((None, 1, D), lambda i, f: (layer, 0, 0)),
            pl.BlockSpec((None, D, tf), lambda i, f: (layer, 0, f)),
            pl.BlockSpec((None, D, tf), lambda i, f: (layer, 0, f)),
            pl.BlockSpec((None, tf, D), lambda i, f: (layer, f, 0)),
            pl.BlockSpec((1, D), lambda i, f: (0, 0)),
        ],
        out_specs=pl.BlockSpec((tm, D), lambda i, f: (i, 0)),
        out_shape=jax.ShapeDtypeStruct((T, D), F32),
        scratch_shapes=[pltpu.VMEM((tm, D), BF16)],
        compiler_params=_params("parallel", "arbitrary"),
        name="ffn",
    )(x, gain, wg, wu, wd, final_gain)


def _rope(x, cos, sin_signed):
    return x * cos + pltpu.roll(x, 2 * HALF, axis=1) * sin_signed


def _split3(x):
    a1 = x.astype(BF16)
    r1 = x - a1.astype(F32)
    a2 = r1.astype(BF16)
    a3 = (r1 - a2.astype(F32)).astype(BF16)
    return a1, a2, a3


def _store_chunks(ref, value):
    for c in range(ref.shape[0]):
        ref[c] = value[:, LANES * c:LANES * (c + 1)]


def _swa_kv_head(hh):
    return hh // (SWA_HEADS // SWA_KV_HEADS)


def _swa_scores(q, k_heads):
    lane = lax.broadcasted_iota(jnp.int32, (1, LANES), 1)
    pair_lo = (lane % (2 * HALF)) < HALF
    zero = jnp.zeros((), BF16)
    scores = []
    for hh in range(SWA_HEADS):
        kg = k_heads[_swa_kv_head(hh)]
        ke = jnp.where(pair_lo, kg, zero) if hh % 2 == 0 else jnp.where(pair_lo, zero, kg)
        scores.append(_qk(q[:, LANES * (hh // 2):LANES * (hh // 2 + 1)], ke))
    return scores


def _swa_finish(sink_ref, scores, v_heads, has_prev):
    t = scores[0].shape[0]
    lane = lax.broadcasted_iota(jnp.int32, (1, LANES), 1)
    row = lax.broadcasted_iota(jnp.int32, (t, WINDOW + t), 0)
    col = lax.broadcasted_iota(jnp.int32, (t, WINDOW + t), 1)
    valid = (col <= row + WINDOW) & (col > row) & ((col >= WINDOW) | has_prev)
    probs, denoms = [], []
    for hh in range(SWA_HEADS):
        sink = sink_ref[hh] * LOG2E
        s = jnp.where(valid, scores[hh], NEG)
        m = jnp.maximum(jnp.max(s, axis=-1, keepdims=True), sink)
        p = jnp.exp2(s - m)
        denoms.append(jnp.sum(p, axis=-1, keepdims=True) + jnp.exp2(sink - m))
        probs.append(p.astype(BF16))
    outs = []
    for hh in range(SWA_HEADS):
        outs.append(jnp.dot(probs[hh], v_heads[_swa_kv_head(hh)], preferred_element_type=F32) / denoms[hh])
    return jnp.concatenate([jnp.where(lane < SWA_DIM, outs[2 * c], outs[2 * c + 1]) for c in range(SWA_HEADS // 2)],
                           axis=1).astype(BF16)


def _mixin_kernel(x_ref, g_ref, win_ref, qn_ref, wqb_ref, kvn_ref, wkvb_ref, cos_ref, sin_ref, fb_ref, sink_ref,
                  qmn_ref, qmp_ref, kmn_ref, kmr_ref, vm_ref, os_ref,
                  qf_ref, kf_ref, vf_ref, qaug_ref, kaug_ref, carry_sc, kprev_sc, vprev_sc, *, steps_per_seq):
    i = pl.program_id(0)
    tm = x_ref.shape[0]
    seq_start = i % steps_per_seq == 0

    @pl.when(seq_start)
    def _():
        carry_sc[...] = jnp.zeros_like(carry_sc)
        kprev_sc[...] = jnp.zeros_like(kprev_sc)
        vprev_sc[...] = jnp.zeros_like(vprev_sc)

    h = _rms(x_ref[...], g_ref[...]).astype(BF16)
    cos = cos_ref[...]
    sin = sin_ref[...]
    lane = lax.broadcasted_iota(jnp.int32, (1, LANES), 1)
    n_nope = MLA_HEADS * MLA_NOPE

    def proj(a, b):
        return jnp.dot(h, win_ref[:, a:b], preferred_element_type=F32)

    cq = _rms(proj(_O_CQ, _O_CKV), qn_ref[...]).astype(BF16)
    ckv = _rms(proj(_O_CKV, _O_KR), kvn_ref[...]).astype(BF16)
    fl = proj(_O_FG, IN_COLS_R) + fb_ref[...]
    logf = jnp.minimum(fl, 0.0) - jnp.log(1.0 + jnp.exp(-jnp.abs(fl)))

    kmr_ref[...] = _rope(proj(_O_KR, _O_QS), cos, sin).astype(BF16)
    qs = proj(_O_QS, _O_KS)
    q_swa = jnp.concatenate([(_rope(qs[:, LANES * c:LANES * (c + 1)], cos, sin) * (SWA_DIM ** -0.5 * LOG2E)).astype(BF16)
                             for c in range(SWA_HEADS // 2)], axis=1)
    ks = proj(_O_KS, _O_VS)
    vs = proj(_O_VS, _O_QF)
    k_heads, v_heads = [], []
    for g in range(SWA_KV_HEADS):
        sl = slice(LANES * g, LANES * (g + 1))
        k_own = _rope(ks[:, sl], cos, sin).astype(BF16)
        v_own = vs[:, sl].astype(BF16)
        k_heads.append(jnp.concatenate([kprev_sc[:, sl], k_own], axis=0))
        v_heads.append(jnp.concatenate([vprev_sc[:, sl], v_own], axis=0))
        kprev_sc[:, sl] = k_own[tm - WINDOW:, :]
        vprev_sc[:, sl] = v_own[tm - WINDOW:, :]
    swa_scores = _swa_scores(q_swa, k_heads)

    q = jnp.dot(cq, wqb_ref[...], preferred_element_type=F32)
    q_scale = (MLA_NOPE + MLA_ROPE) ** -0.5 * LOG2E
    _store_chunks(qmn_ref, (q[:, :n_nope] * q_scale).astype(BF16))
    for c in range(MLA_HEADS // 2):
        qmp_ref[c] = (_rope(q[:, n_nope + LANES * c:n_nope + LANES * (c + 1)], cos, sin) * q_scale).astype(BF16)
    kv = jnp.dot(ckv, wkvb_ref[...], preferred_element_type=F32)
    _store_chunks(kmn_ref, kv[:, :n_nope].astype(BF16))
    _store_chunks(vm_ref, kv[:, n_nope:].astype(BF16))

    _store_chunks(qf_ref, (proj(_O_QF, _O_KF) * (FOX_DIM ** -0.5 * LOG2E)).astype(BF16))
    _store_chunks(kf_ref, proj(_O_KF, _O_VF).astype(BF16))
    _store_chunks(vf_ref, proj(_O_VF, _O_FG).astype(BF16))

    row = lax.broadcasted_iota(jnp.int32, (tm, tm), 0)
    col = lax.broadcasted_iota(jnp.int32, (tm, tm), 1)
    tri = (col <= row).astype(BF16)
    csum = carry_sc[...]
    for term in _split3(logf):
        csum = csum + jnp.dot(tri, term, preferred_element_type=F32)
    carry_sc[...] = csum[tm - 1:tm, :]

    c1, c2, c3 = (term.astype(F32) for term in _split3(csum * LOG2E))
    g8 = lane // FOX_HEADS
    ones = jnp.where(g8 < GATE_COPIES, 1.0, 0.0)
    qaug_ref[...] = jnp.where(g8 < 3, ones, jnp.where(g8 == 3, c1, jnp.where(g8 == 4, c2, jnp.where(g8 == 5, c3, 0.0)))).astype(BF16)
    kaug_ref[...] = jnp.where(g8 == 0, -c1, jnp.where(g8 == 1, -c2, jnp.where(g8 == 2, -c3, ones))).astype(BF16)

    os_ref[...] = _swa_finish(sink_ref, swa_scores, v_heads, i % steps_per_seq > 0)


def _mixin(x, gain, w_in, qn, wqb, kvn, wkvb, cos, sin, fb, sinks, layer, *, batch, tm=256):
    T, D = x.shape
    S = T // batch
    assert S % tm == 0 and tm % WINDOW == 0
    sps = S // tm

    def row(width):
        return pl.BlockSpec((tm, width), lambda i: (i, 0))

    def wres(a):
        return _resident((None,) + a.shape[1:], lambda i: (layer, 0, 0))

    def chunked(chunks):
        return (jax.ShapeDtypeStruct((batch, chunks, S, LANES), BF16),
                pl.BlockSpec((None, chunks, tm, LANES), lambda i: (i // sps, 0, i % sps, 0)))

    def flat(width):
        return jax.ShapeDtypeStruct((T, width), BF16), row(width)

    outs = dict(qmn=chunked(MLA_HEADS), qmp=chunked(MLA_HEADS // 2), kmn=chunked(MLA_HEADS), kmr=flat(LANES),
                vm=chunked(MLA_HEADS), o_swa=flat(SWA_DIM * SWA_HEADS), qf=chunked(FOX_HEADS // 2),
                kf=chunked(FOX_HEADS // 2), vf=chunked(FOX_HEADS // 2), qaug=flat(LANES), kaug=flat(LANES))
    kv_w = LANES * SWA_KV_HEADS
    return pl.pallas_call(
        functools.partial(_mixin_kernel, steps_per_seq=sps),
        grid=(T // tm,),
        in_specs=[row(D), wres(gain), wres(w_in), wres(qn), wres(wqb), wres(kvn), wres(wkvb),
                  row(LANES), row(LANES), wres(fb), pl.BlockSpec(memory_space=pltpu.SMEM)],
        out_specs=[spec for _, spec in outs.values()],
        out_shape=[shape for shape, _ in outs.values()],
        scratch_shapes=[pltpu.VMEM((1, LANES), F32), pltpu.VMEM((WINDOW, kv_w), BF16), pltpu.VMEM((WINDOW, kv_w), BF16)],
        compiler_params=_params("arbitrary"),
        name="mixin",
    )(x, gain, w_in, qn, wqb, kvn, wkvb, cos, sin, fb, sinks)


def _qk(q, k):
    return lax.dot_general(q, k, (((1,), (1,)), ((), ())), preferred_element_type=F32)


def _causal_mask(t):
    row = lax.broadcasted_iota(jnp.int32, (t, t), 0)
    col = lax.broadcasted_iota(jnp.int32, (t, t), 1)
    return col <= row


def _flash_kernel(tab_ref, qa_ref, qb_ref, ka_ref, kb_ref, v_ref, o_ref,
                  q_sc, k_sc, v_sc, m_sc, acc_sc, *bufs, kind, t, n_off):
    h = pl.program_id(1)
    n = q_sc.shape[0] // t
    lane = lax.broadcasted_iota(jnp.int32, (1, LANES), 1)
    even = (h % 2) == 0
    zero = jnp.zeros((), BF16)

    if kind == "mla":
        mine = ((lane % (2 * HALF)) < HALF) == even
        q_sc[:, :LANES] = qa_ref[...]
        q_sc[:, LANES:] = qb_ref[...]
        k_sc[:, :LANES] = ka_ref[...]
        k_sc[:, LANES:] = jnp.where(mine, kb_ref[...], zero)
        v_sc[:, :LANES] = v_ref[...]
    else:
        mine = (lane < FOX_DIM) == even
        gate = ((lane % FOX_HEADS) == h) & (lane < GATE_COPIES * FOX_HEADS)
        q_sc[:, :LANES] = qa_ref[...]
        q_sc[:, LANES:] = jnp.where(gate, qb_ref[...], zero)
        k_sc[:, :LANES] = jnp.where(mine, ka_ref[...], zero)
        k_sc[:, LANES:] = kb_ref[...]
        v_sc[:, :LANES] = jnp.where(mine, v_ref[...], zero)
    v_sc[:, LANES:] = jnp.ones((v_sc.shape[0], LANES), BF16)

    def rows(j):
        return pl.ds(pl.multiple_of(j * t, t), t)

    s_bufs, p_bufs, a_bufs, r_bufs = (bufs[FLASH_BUFS * g:FLASH_BUFS * (g + 1)] for g in range(4))

    def scores(i, j, slot):
        s = _qk(q_sc[rows(i), :], k_sc[rows(j), :])
        s_bufs[slot][...] = s
        r_bufs[slot][...] = functools.reduce(jnp.maximum, [s[:, LANES * c:LANES * (c + 1)] for c in range(t // LANES)])

    def softmax(par, i, first):
        s = s_bufs[par][...]
        if first:
            s = jnp.where(_causal_mask(t), s, NEG)
        tiles = [s[:, LANES * c:LANES * (c + 1)] for c in range(t // LANES)]
        tile_max = functools.reduce(jnp.maximum, tiles) if first else r_bufs[par][...]
        m_new = jnp.broadcast_to(jnp.max(tile_max, axis=-1, keepdims=True), (t, LANES))
        if not first:
            m_old = m_sc[rows(i), :]
            m_new = jnp.maximum(m_old, m_new)
            a_bufs[par][...] = jnp.exp2(m_old - m_new)
        for c, tile in enumerate(tiles):
            p_bufs[par][:, LANES * c:LANES * (c + 1)] = jnp.exp2(tile - m_new).astype(BF16)
        m_sc[rows(i), :] = m_new

    def accumulate(par, i, j, first):
        pv = jnp.dot(p_bufs[par][...], v_sc[rows(j), :], preferred_element_type=F32)
        if first:
            acc_sc[rows(i), :] = pv
        else:
            a = a_bufs[par][...]
            acc_sc[rows(i), :] = jnp.concatenate([a, a], axis=1) * acc_sc[rows(i), :] + pv

    def pipeline(base, count, first, lag, unroll=1):
        nb = 2 * lag
        blk = lambda a: (tab_ref[0, base + a], tab_ref[1, base + a])

        def step(a, u, with_acc=True):
            if with_acc:
                accumulate((u - lag) % nb, *blk(a - lag), first)
            scores(*blk(a + lag), (u + lag) % nb)
            softmax(u, blk(a)[0], first)

        for a in range(lag):
            scores(*blk(a), a)
        for a in range(min(lag, count)):
            step(a, a, with_acc=False)
        main = max(count - lag, 0)

        span = nb * unroll

        def body(k, carry):
            for u in range(span):
                step(lag + span * k + u, (lag + u) % nb)
            return carry

        lax.fori_loop(0, main // span, body, 0)
        for a in range(lag + main // span * span, count):
            step(a, a % nb)
        for a in range(main, count):
            accumulate(a % nb, *blk(a), first)

    pipeline(0, n, True, 1)
    pipeline(n, n_off, False, FLASH_LAG, unroll=FLASH_UNROLL)

    def finish(i, carry):
        o_ref[rows(i), :] = (acc_sc[rows(i), :LANES] / acc_sc[rows(i), LANES:]).astype(o_ref.dtype)
        return carry

    lax.fori_loop(0, n, finish, 0)


def _flash_attention(qa, qb, ka, kb, v, *, kind, t=512):
    B, _, S, _ = qa.shape
    heads = MLA_HEADS if kind == "mla" else FOX_HEADS
    n = S // t
    assert S % t == 0 and n % 2 == 0
    diag = [(i, i) for i in range(n)]
    off = [(i, j) for j in range(n - 1) for i in range(j + 1, n)]
    table = jnp.asarray(np.array(diag + off + off[-1:] * (2 * FLASH_LAG), np.int32).T)
    shared = lambda **kw: pl.BlockSpec((None, S, LANES), lambda b, h: (b, 0, 0), **kw)
    own = lambda **kw: pl.BlockSpec((None, None, S, LANES), lambda b, h: (b, h, 0, 0), **kw)
    pair = lambda **kw: pl.BlockSpec((None, None, S, LANES), lambda b, h: (b, h // 2, 0, 0), **kw)
    single = dict(pipeline_mode=pl.Buffered(1))
    if kind == "mla":
        in_specs = [own(**single), pair(**single), own(), shared(), own()]
    else:
        in_specs = [pair(**single), shared(**single), pair(), shared(), pair()]
    return pl.pallas_call(
        functools.partial(_flash_kernel, kind=kind, t=t, n_off=len(off)),
        grid=(B, heads),
        in_specs=[pl.BlockSpec(memory_space=pltpu.SMEM)] + in_specs,
        out_specs=own(),
        out_shape=jax.ShapeDtypeStruct((B, heads, S, LANES), BF16),
        scratch_shapes=[pltpu.VMEM((S, 2 * LANES), BF16), pltpu.VMEM((S, 2 * LANES), BF16),
                        pltpu.VMEM((S, 2 * LANES), BF16),
                        pltpu.VMEM((S, LANES), F32), pltpu.VMEM((S, 2 * LANES), F32)]
                       + [pltpu.VMEM((t, t), F32)] * FLASH_BUFS
                       + [pltpu.VMEM((t, t), BF16)] * FLASH_BUFS
                       + [pltpu.VMEM((t, LANES), F32)] * FLASH_BUFS
                       + [pltpu.VMEM((t, LANES), F32)] * FLASH_BUFS,
        compiler_params=_params("parallel", "arbitrary"),
        name=kind + "_attn",
    )(table, qa, qb, ka, kb, v)


def _outproj_kernel(x_ref, om_ref, os_ref, of_ref, w_ref, o_ref):
    a = om_ref.shape[0] * LANES
    b = a + os_ref.shape[1]
    om = jnp.concatenate([om_ref[h] for h in range(om_ref.shape[0])], axis=1)
    y = jnp.dot(om, w_ref[:a, :], preferred_element_type=F32)
    y += jnp.dot(os_ref[...], w_ref[a:b, :], preferred_element_type=F32)
    of = jnp.concatenate([of_ref[2 * c] + of_ref[2 * c + 1] for c in range(of_ref.shape[0] // 2)], axis=1)
    y += jnp.dot(of, w_ref[b:, :], preferred_element_type=F32)
    o_ref[...] = x_ref[...] + y


def _outproj(x, om, osw, of, w_out, layer, *, tm=512):
    T, D = x.shape
    S = om.shape[2]
    assert S % tm == 0
    sps = S // tm

    def row(width):
        return pl.BlockSpec((tm, width), lambda i: (i, 0))

    def heads(a):
        return pl.BlockSpec((None, a.shape[1], tm, LANES), lambda i: (i // sps, 0, i % sps, 0))

    return pl.pallas_call(
        _outproj_kernel,
        grid=(T // tm,),
        in_specs=[row(D), heads(om), row(osw.shape[1]), heads(of),
                  _resident((None,) + w_out.shape[1:], lambda i: (layer, 0, 0))],
        out_specs=row(D),
        out_shape=jax.ShapeDtypeStruct((T, D), F32),
        compiler_params=_params("parallel"),
        name="outproj",
    )(x, om, osw, of, w_out)


def kernel(x, positions, ffn1_norm, ffn1_w_gate, ffn1_w_up, ffn1_w_down, mix_norm, w_in, mla_q_norm, mla_w_q_b, mla_kv_norm, mla_w_kv_b, swa_sinks, fox_forget_bias, w_out, ffn2_norm, ffn2_w_gate, ffn2_w_up, ffn2_w_down, final_norm):
    B, S, D = x.shape
    depth = w_in.shape[0]
    T = B * S

    bf = lambda w: w.astype(BF16)
    w_in_r = _take_cols(w_in, _IN_PERM)
    wqb_r = _take_cols(mla_w_q_b, _QB_PERM)
    wkvb_r = _take_cols(mla_w_kv_b, _KVB_PERM)
    ffn_w = [(bf(ffn1_w_gate), bf(ffn1_w_up), bf(ffn1_w_down)), (bf(ffn2_w_gate), bf(ffn2_w_up), bf(ffn2_w_down))]
    ffn_g = [ffn1_norm[:, None, :], ffn2_norm[:, None, :]]
    w_out_b = bf(w_out)
    mix_g = mix_norm[:, None, :]
    qn = mla_q_norm[:, None, :]
    kvn = mla_kv_norm[:, None, :]
    fb = jnp.pad(jnp.tile(fox_forget_bias, (1, GATE_COPIES)), ((0, 0), (0, LANES - GATE_COPIES * FOX_HEADS)))[:, None, :]
    fin_g = final_norm[None, :]

    inv_freq = ROPE_THETA ** (-jnp.arange(0, 2 * HALF, 2, dtype=F32) / (2 * HALF))
    ang = positions.astype(F32).reshape(T, 1) * jnp.tile(inv_freq, LANES // HALF)
    sign = jnp.where(jnp.arange(LANES) < LANES // 2, -1.0, 1.0).astype(F32)
    cos_t = jnp.cos(ang)
    sin_t = jnp.sin(ang) * sign

    xt = x.reshape(T, D)
    sh = lambda a: a.reshape(B, S, a.shape[-1])
    for l in range(depth):
        xt = _ffn(xt, ffn_g[0], *ffn_w[0], l, fin_g, final_norm=False)
        qmn, qmp, kmn, kmr, vm, o_swa, qf, kf, vf, qaug, kaug = _mixin(
            xt, mix_g, w_in_r, qn, wqb_r, kvn, wkvb_r, cos_t, sin_t, fb, swa_sinks[l], l, batch=B)
        o_mla = _flash_attention(qmn, qmp, kmn, sh(kmr), vm, kind="mla")
        o_fox = _flash_attention(qf, sh(qaug), kf, sh(kaug), vf, kind="fox")
        xt = _outproj(xt, o_mla, o_swa, o_fox, w_out_b, l)
        xt = _ffn(xt, ffn_g[1], *ffn_w[1], l, fin_g, final_norm=(l == depth - 1))
    return xt.reshape(B, S, D)
```

```python
import functools
import math

import numpy as np
import jax
import jax.numpy as jnp
from jax import lax
from jax.experimental import pallas as pl
from jax.experimental.pallas import tpu as pltpu

RMS_EPS = 1e-6
ROPE_THETA = 10000.0

MLA_HEADS = 8
MLA_Q_LORA = 512
MLA_KV_LORA = 256
MLA_NOPE = 128
MLA_ROPE = 64
MLA_V = 128

SWA_HEADS = 8
SWA_KV_HEADS = 2
SWA_DIM = 64
WINDOW = 128

FOX_HEADS = 8
FOX_DIM = 64

LANES = 128
HALF = 32
VMEM_LIMIT = 60 * 1024 * 1024
FFN_ROW_CHUNK = 256
FLASH_LAG = 2
FLASH_BUFS = 2 * FLASH_LAG
FLASH_UNROLL = 7
NEG = -1e30
LOG2E = math.log2(math.e)
GATE_COPIES = 6

F32 = jnp.float32
BF16 = jnp.bfloat16

_O_CQ, _O_CKV, _O_KR, _O_QS, _O_KS, _O_VS, _O_QF, _O_KF, _O_VF, _O_FG, IN_COLS_R = (
    0, 512, 768, 896, 1408, 1664, 1920, 2432, 2944, 3456, 3584)


def _chunk_pair(base_a, base_b):
    r = np.arange(HALF)
    return np.concatenate([base_a + r, base_b + r, base_a + HALF + r, base_b + HALF + r])


def _in_perm():
    src = [0, 512, 768, 832, 1344, 1472, 1600, 2112, 2624, 3136]
    zero = 3144
    cols = [np.arange(src[0], src[0] + 512), np.arange(src[1], src[1] + 256)]
    cols.append(_chunk_pair(src[2], src[2]))
    for c in range(SWA_HEADS // 2):
        cols.append(_chunk_pair(src[3] + SWA_DIM * 2 * c, src[3] + SWA_DIM * (2 * c + 1)))
    for g in range(SWA_KV_HEADS):
        cols.append(_chunk_pair(src[4] + SWA_DIM * g, src[4] + SWA_DIM * g))
    for g in range(SWA_KV_HEADS):
        v = src[5] + SWA_DIM * g + np.arange(SWA_DIM)
        cols.append(np.concatenate([v, v]))
    cols += [np.arange(src[6], src[6] + 512), np.arange(src[7], src[7] + 512), np.arange(src[8], src[8] + 512)]
    gate = np.tile(np.arange(src[9], src[9] + FOX_HEADS), GATE_COPIES)
    cols.append(np.concatenate([gate, np.full(LANES - gate.shape[0], zero)]))
    out = np.concatenate(cols).astype(np.int32)
    assert out.shape[0] == IN_COLS_R
    return out


def _qb_perm():
    per = MLA_NOPE + MLA_ROPE
    cols = [per * h + np.arange(MLA_NOPE) for h in range(MLA_HEADS)]
    for c in range(MLA_HEADS // 2):
        cols.append(_chunk_pair(per * 2 * c + MLA_NOPE, per * (2 * c + 1) + MLA_NOPE))
    return np.concatenate(cols).astype(np.int32)


def _kvb_perm():
    per = MLA_NOPE + MLA_V
    k = [per * h + np.arange(MLA_NOPE) for h in range(MLA_HEADS)]
    v = [per * h + MLA_NOPE + np.arange(MLA_V) for h in range(MLA_HEADS)]
    return np.concatenate(k + v).astype(np.int32)


_IN_PERM, _QB_PERM, _KVB_PERM = _in_perm(), _qb_perm(), _kvb_perm()


def _take_cols(w, perm):
    ncol = w.shape[-1]
    same_run = lambda k: (perm[k] == ncol) if perm[k - 1] == ncol else (perm[k] == perm[k - 1] + 1 and perm[k] != ncol)
    cuts = [0] + [k for k in range(1, len(perm)) if not same_run(k)] + [len(perm)]
    parts = []
    for a, b in zip(cuts[:-1], cuts[1:]):
        if perm[a] == ncol:
            parts.append(jnp.zeros(w.shape[:-1] + (b - a,), w.dtype))
        else:
            parts.append(w[..., int(perm[a]):int(perm[a]) + (b - a)])
    return jnp.concatenate(parts, axis=-1).astype(BF16)


def _params(*sem):
    return pltpu.CompilerParams(dimension_semantics=sem, vmem_limit_bytes=VMEM_LIMIT)


def _resident(block_shape, index_map):
    return pl.BlockSpec(block_shape, index_map, pipeline_mode=pl.Buffered(1))


def _rms(x, gain):
    ms = jnp.mean(x * x, axis=-1, keepdims=True)
    return x * lax.rsqrt(ms + RMS_EPS) * gain


def _ffn_kernel(x_ref, g_ref, wg_ref, wu_ref, wd_ref, fg_ref, o_ref, h_sc, *, final_norm):
    f = pl.program_id(1)

    def row_chunks(body):
        def step(r, carry):
            body(pl.ds(pl.multiple_of(r * FFN_ROW_CHUNK, FFN_ROW_CHUNK), FFN_ROW_CHUNK))
            return carry
        lax.fori_loop(0, x_ref.shape[0] // FFN_ROW_CHUNK, step, 0)

    @pl.when(f == 0)
    def _():
        def prologue(rows):
            h_sc[rows, :] = _rms(x_ref[rows, :], g_ref[...]).astype(BF16)
            o_ref[rows, :] = jnp.zeros((FFN_ROW_CHUNK, o_ref.shape[1]), F32)
        row_chunks(prologue)

    h = h_sc[...]
    gate = jnp.dot(h, wg_ref[...], preferred_element_type=F32)
    up = jnp.dot(h, wu_ref[...], preferred_element_type=F32)
    act = (gate * (1.0 / (1.0 + jnp.exp(-gate))) * up).astype(BF16)
    o_ref[...] += jnp.dot(act, wd_ref[...], preferred_element_type=F32)

    @pl.when(f == pl.num_programs(1) - 1)
    def _():
        def epilogue(rows):
            y = x_ref[rows, :] + 0.5 * o_ref[rows, :]
            if final_norm:
                y = _rms(y, fg_ref[...])
            o_ref[rows, :] = y
        row_chunks(epilogue)


def _ffn(x, gain, wg, wu, wd, layer, final_gain, *, final_norm, tm=1024, tf=512):
    T, D = x.shape
    F = wg.shape[-1]
    assert T % tm == 0 and F % tf == 0
    return pl.pallas_call(
        functools.partial(_ffn_kernel, final_norm=final_norm),
        grid=(T // tm, F // tf),
        in_specs=[
            pl.BlockSpec((tm, D), lambda i, f: (i, 0)),
            pl.BlockSpec((None, 1, D), lambda i, f: (layer, 0, 0)),
            pl.BlockSpec((None, D, tf), lambda i, f: (layer, 0, f)),
            pl.BlockSpec((None, D, tf), lambda i, f: (layer, 0, f)),
            pl.BlockSpec((None, tf, D), lambda i, f: (layer, f, 0)),
            pl.BlockSpec((1, D), lambda i, f: (0, 0)),
        ],
        out_specs=pl.BlockSpec((tm, D), lambda i, f: (i, 0)),
        out_shape=jax.ShapeDtypeStruct((T, D), F32),
        scratch_shapes=[pltpu.VMEM((tm, D), BF16)],
        compiler_params=_params("parallel", "arbitrary"),
        name="ffn",
    )(x, gain, wg, wu, wd, final_gain)


def _rope(x, cos, sin_signed):
    return x * cos + pltpu.roll(x, 2 * HALF, axis=1) * sin_signed


def _split3(x):
    a1 = x.astype(BF16)
    r1 = x - a1.astype(F32)
    a2 = r1.astype(BF16)
    a3 = (r1 - a2.astype(F32)).astype(BF16)
    return a1, a2, a3


def _store_chunks(ref, value):
    for c in range(ref.shape[0]):
        ref[c] = value[:, LANES * c:LANES * (c + 1)]


def _swa_kv_head(hh):
    return hh // (SWA_HEADS // SWA_KV_HEADS)


def _swa_scores(q, k_heads):
    lane = lax.broadcasted_iota(jnp.int32, (1, LANES), 1)
    pair_lo = (lane % (2 * HALF)) < HALF
    zero = jnp.zeros((), BF16)
    scores = []
    for hh in range(SWA_HEADS):
        kg = k_heads[_swa_kv_head(hh)]
        ke = jnp.where(pair_lo, kg, zero) if hh % 2 == 0 else jnp.where(pair_lo, zero, kg)
        scores.append(_qk(q[:, LANES * (hh // 2):LANES * (hh // 2 + 1)], ke))
    return scores


def _swa_finish(sink_ref, scores, v_heads, has_prev):
    t = scores[0].shape[0]
    lane = lax.broadcasted_iota(jnp.int32, (1, LANES), 1)
    row = lax.broadcasted_iota(jnp.int32, (t, WINDOW + t), 0)
    col = lax.broadcasted_iota(jnp.int32, (t, WINDOW + t), 1)
    valid = (col <= row + WINDOW) & (col > row) & ((col >= WINDOW) | has_prev)
    probs, denoms = [], []
    for hh in range(SWA_HEADS):
        sink = sink_ref[hh] * LOG2E
        s = jnp.where(valid, scores[hh], NEG)
        m = jnp.maximum(jnp.max(s, axis=-1, keepdims=True), sink)
        p = jnp.exp2(s - m)
        denoms.append(jnp.sum(p, axis=-1, keepdims=True) + jnp.exp2(sink - m))
        probs.append(p.astype(BF16))
    outs = []
    for hh in range(SWA_HEADS):
        outs.append(jnp.dot(probs[hh], v_heads[_swa_kv_head(hh)], preferred_element_type=F32) / denoms[hh])
    return jnp.concatenate([jnp.where(lane < SWA_DIM, outs[2 * c], outs[2 * c + 1]) for c in range(SWA_HEADS // 2)],
                           axis=1).astype(BF16)


def _mixin_kernel(x_ref, g_ref, win_ref, qn_ref, wqb_ref, kvn_ref, wkvb_ref, cos_ref, sin_ref, fb_ref, sink_ref,
                  qmn_ref, qmp_ref, kmn_ref, kmr_ref, vm_ref, os_ref,
                  qf_ref, kf_ref, vf_ref, qaug_ref, kaug_ref, carry_sc, kprev_sc, vprev_sc, *, steps_per_seq):
    i = pl.program_id(0)
    tm = x_ref.shape[0]
    seq_start = i % steps_per_seq == 0

    @pl.when(seq_start)
    def _():
        carry_sc[...] = jnp.zeros_like(carry_sc)
        kprev_sc[...] = jnp.zeros_like(kprev_sc)
        vprev_sc[...] = jnp.zeros_like(vprev_sc)

    h = _rms(x_ref[...], g_ref[...]).astype(BF16)
    cos = cos_ref[...]
    sin = sin_ref[...]
    lane = lax.broadcasted_iota(jnp.int32, (1, LANES), 1)
    n_nope = MLA_HEADS * MLA_NOPE

    def proj(a, b):
        return jnp.dot(h, win_ref[:, a:b], preferred_element_type=F32)

    cq = _rms(proj(_O_CQ, _O_CKV), qn_ref[...]).astype(BF16)
    ckv = _rms(proj(_O_CKV, _O_KR), kvn_ref[...]).astype(BF16)
    fl = proj(_O_FG, IN_COLS_R) + fb_ref[...]
    logf = jnp.minimum(fl, 0.0) - jnp.log(1.0 + jnp.exp(-jnp.abs(fl)))

    kmr_ref[...] = _rope(proj(_O_KR, _O_QS), cos, sin).astype(BF16)
    qs = proj(_O_QS, _O_KS)
    q_swa = jnp.concatenate([(_rope(qs[:, LANES * c:LANES * (c + 1)], cos, sin) * (SWA_DIM ** -0.5 * LOG2E)).astype(BF16)
                             for c in range(SWA_HEADS // 2)], axis=1)
    ks = proj(_O_KS, _O_VS)
    vs = proj(_O_VS, _O_QF)
    k_heads, v_heads = [], []
    for g in range(SWA_KV_HEADS):
        sl = slice(LANES * g, LANES * (g + 1))
        k_own = _rope(ks[:, sl], cos, sin).astype(BF16)
        v_own = vs[:, sl].astype(BF16)
        k_heads.append(jnp.concatenate([kprev_sc[:, sl], k_own], axis=0))
        v_heads.append(jnp.concatenate([vprev_sc[:, sl], v_own], axis=0))
        kprev_sc[:, sl] = k_own[tm - WINDOW:, :]
        vprev_sc[:, sl] = v_own[tm - WINDOW:, :]
    swa_scores = _swa_scores(q_swa, k_heads)

    q = jnp.dot(cq, wqb_ref[...], preferred_element_type=F32)
    q_scale = (MLA_NOPE + MLA_ROPE) ** -0.5 * LOG2E
    _store_chunks(qmn_ref, (q[:, :n_nope] * q_scale).astype(BF16))
    for c in range(MLA_HEADS // 2):
        qmp_ref[c] = (_rope(q[:, n_nope + LANES * c:n_nope + LANES * (c + 1)], cos, sin) * q_scale).astype(BF16)
    kv = jnp.dot(ckv, wkvb_ref[...], preferred_element_type=F32)
    _store_chunks(kmn_ref, kv[:, :n_nope].astype(BF16))
    _store_chunks(vm_ref, kv[:, n_nope:].astype(BF16))

    _store_chunks(qf_ref, (proj(_O_QF, _O_KF) * (FOX_DIM ** -0.5 * LOG2E)).astype(BF16))
    _store_chunks(kf_ref, proj(_O_KF, _O_VF).astype(BF16))
    _store_chunks(vf_ref, proj(_O_VF, _O_FG).astype(BF16))

    row = lax.broadcasted_iota(jnp.int32, (tm, tm), 0)
    col = lax.broadcasted_iota(jnp.int32, (tm, tm), 1)
    tri = (col <= row).astype(BF16)
    csum = carry_sc[...]
    for term in _split3(logf):
        csum = csum + jnp.dot(tri, term, preferred_element_type=F32)
    carry_sc[...] = csum[tm - 1:tm, :]

    c1, c2, c3 = (term.astype(F32) for term in _split3(csum * LOG2E))
    g8 = lane // FOX_HEADS
    ones = jnp.where(g8 < GATE_COPIES, 1.0, 0.0)
    qaug_ref[...] = jnp.where(g8 < 3, ones, jnp.where(g8 == 3, c1, jnp.where(g8 == 4, c2, jnp.where(g8 == 5, c3, 0.0)))).astype(BF16)
    kaug_ref[...] = jnp.where(g8 == 0, -c1, jnp.where(g8 == 1, -c2, jnp.where(g8 == 2, -c3, ones))).astype(BF16)

    os_ref[...] = _swa_finish(sink_ref, swa_scores, v_heads, i % steps_per_seq > 0)


def _mixin(x, gain, w_in, qn, wqb, kvn, wkvb, cos, sin, fb, sinks, layer, *, batch, tm=256):
    T, D = x.shape
    S = T // batch
    assert S % tm == 0 and tm % WINDOW == 0
    sps = S // tm

    def row(width):
        return pl.BlockSpec((tm, width), lambda i: (i, 0))

    def wres(a):
        return _resident((None,) + a.shape[1:], lambda i: (layer, 0, 0))

    def chunked(chunks):
        return (jax.ShapeDtypeStruct((batch, chunks, S, LANES), BF16),
                pl.BlockSpec((None, chunks, tm, LANES), lambda i: (i // sps, 0, i % sps, 0)))

    def flat(width):
        return jax.ShapeDtypeStruct((T, width), BF16), row(width)

    outs = dict(qmn=chunked(MLA_HEADS), qmp=chunked(MLA_HEADS // 2), kmn=chunked(MLA_HEADS), kmr=flat(LANES),
                vm=chunked(MLA_HEADS), o_swa=flat(SWA_DIM * SWA_HEADS), qf=chunked(FOX_HEADS // 2),
                kf=chunked(FOX_HEADS // 2), vf=chunked(FOX_HEADS // 2), qaug=flat(LANES), kaug=flat(LANES))
    kv_w = LANES * SWA_KV_HEADS
    return pl.pallas_call(
        functools.partial(_mixin_kernel, steps_per_seq=sps),
        grid=(T // tm,),
        in_specs=[row(D), wres(gain), wres(w_in), wres(qn), wres(wqb), wres(kvn), wres(wkvb),
                  row(LANES), row(LANES), wres(fb), pl.BlockSpec(memory_space=pltpu.SMEM)],
        out_specs=[spec for _, spec in outs.values()],
        out_shape=[shape for shape, _ in outs.values()],
        scratch_shapes=[pltpu.VMEM((1, LANES), F32), pltpu.VMEM((WINDOW, kv_w), BF16), pltpu.VMEM((WINDOW, kv_w), BF16)],
        compiler_params=_params("arbitrary"),
        name="mixin",
    )(x, gain, w_in, qn, wqb, kvn, wkvb, cos, sin, fb, sinks)


def _qk(q, k):
    return lax.dot_general(q, k, (((1,), (1,)), ((), ())), preferred_element_type=F32)


def _causal_mask(t):
    row = lax.broadcasted_iota(jnp.int32, (t, t), 0)
    col = lax.broadcasted_iota(jnp.int32, (t, t), 1)
    return col <= row


def _flash_kernel(tab_ref, qa_ref, qb_ref, ka_ref, kb_ref, v_ref, o_ref,
                  q_sc, k_sc, v_sc, m_sc, acc_sc, *bufs, kind, t, n_off):
    h = pl.program_id(1)
    n = q_sc.shape[0] // t
    lane = lax.broadcasted_iota(jnp.int32, (1, LANES), 1)
    even = (h % 2) == 0
    zero = jnp.zeros((), BF16)

    if kind == "mla":
        mine = ((lane % (2 * HALF)) < HALF) == even
        q_sc[:, :LANES] = qa_ref[...]
        q_sc[:, LANES:] = qb_ref[...]
        k_sc[:, :LANES] = ka_ref[...]
        k_sc[:, LANES:] = jnp.where(mine, kb_ref[...], zero)
        v_sc[:, :LANES] = v_ref[...]
    else:
        mine = (lane < FOX_DIM) == even
        gate = ((lane % FOX_HEADS) == h) & (lane < GATE_COPIES * FOX_HEADS)
        q_sc[:, :LANES] = qa_ref[...]
        q_sc[:, LANES:] = jnp.where(gate, qb_ref[...], zero)
        k_sc[:, :LANES] = jnp.where(mine, ka_ref[...], zero)
        k_sc[:, LANES:] = kb_ref[...]
        v_sc[:, :LANES] = jnp.where(mine, v_ref[...], zero)
    v_sc[:, LANES:] = jnp.ones((v_sc.shape[0], LANES), BF16)

    def rows(j):
        return pl.ds(pl.multiple_of(j * t, t), t)

    s_bufs, p_bufs, a_bufs, r_bufs = (bufs[FLASH_BUFS * g:FLASH_BUFS * (g + 1)] for g in range(4))

    def scores(i, j, slot):
        s = _qk(q_sc[rows(i), :], k_sc[rows(j), :])
        s_bufs[slot][...] = s
        r_bufs[slot][...] = functools.reduce(jnp.maximum, [s[:, LANES * c:LANES * (c + 1)] for c in range(t // LANES)])

    def softmax(par, i, first):
        s = s_bufs[par][...]
        if first:
            s = jnp.where(_causal_mask(t), s, NEG)
        tiles = [s[:, LANES * c:LANES * (c + 1)] for c in range(t // LANES)]
        tile_max = functools.reduce(jnp.maximum, tiles) if first else r_bufs[par][...]
        m_new = jnp.broadcast_to(jnp.max(tile_max, axis=-1, keepdims=True), (t, LANES))
        if not first:
            m_old = m_sc[rows(i), :]
            m_new = jnp.maximum(m_old, m_new)
            a_bufs[par][...] = jnp.exp2(m_old - m_new)
        for c, tile in enumerate(tiles):
            p_bufs[par][:, LANES * c:LANES * (c + 1)] = jnp.exp2(tile - m_new).astype(BF16)
        m_sc[rows(i), :] = m_new

    def accumulate(par, i, j, first):
        pv = jnp.dot(p_bufs[par][...], v_sc[rows(j), :], preferred_element_type=F32)
        if first:
            acc_sc[rows(i), :] = pv
        else:
            a = a_bufs[par][...]
            acc_sc[rows(i), :] = jnp.concatenate([a, a], axis=1) * acc_sc[rows(i), :] + pv

    def pipeline(base, count, first, lag, unroll=1):
        nb = 2 * lag
        blk = lambda a: (tab_ref[0, base + a], tab_ref[1, base + a])

        def step(a, u, with_acc=True):
            scores(*blk(a + lag), (u + lag) % nb)
            if with_acc:
                accumulate((u - lag) % nb, *blk(a - lag), first)
            softmax(u, blk(a)[0], first)

        for a in range(lag):
            scores(*blk(a), a)
        for a in range(min(lag, count)):
            step(a, a, with_acc=False)
        main = max(count - lag, 0)

        span = nb * unroll

        def body(k, carry):
            for u in range(span):
                step(lag + span * k + u, (lag + u) % nb)
            return carry

        lax.fori_loop(0, main // span, body, 0)
        for a in range(lag + main // span * span, count):
            step(a, a % nb)
        for a in range(main, count):
            accumulate(a % nb, *blk(a), first)

    pipeline(0, n, True, 1)
    pipeline(n, n_off, False, FLASH_LAG, unroll=FLASH_UNROLL)

    def finish(i, carry):
        o_ref[rows(i), :] = (acc_sc[rows(i), :LANES] / acc_sc[rows(i), LANES:]).astype(o_ref.dtype)
        return carry

    lax.fori_loop(0, n, finish, 0)


def _flash_attention(qa, qb, ka, kb, v, *, kind, t=512):
    B, _, S, _ = qa.shape
    heads = MLA_HEADS if kind == "mla" else FOX_HEADS
    n = S // t
    assert S % t == 0 and n % 2 == 0
    diag = [(i, i) for i in range(n)]
    off = [(i, j) for j in range(n - 1) for i in range(j + 1, n)]
    table = jnp.asarray(np.array(diag + off + off[-1:] * (2 * FLASH_LAG), np.int32).T)
    shared = lambda **kw: pl.BlockSpec((None, S, LANES), lambda b, h: (b, 0, 0), **kw)
    own = lambda **kw: pl.BlockSpec((None, None, S, LANES), lambda b, h: (b, h, 0, 0), **kw)
    pair = lambda **kw: pl.BlockSpec((None, None, S, LANES), lambda b, h: (b, h // 2, 0, 0), **kw)
    single = dict(pipeline_mode=pl.Buffered(1))
    if kind == "mla":
        in_specs = [own(**single), pair(**single), own(), shared(), own()]
    else:
        in_specs = [pair(**single), shared(**single), pair(), shared(), pair()]
    return pl.pallas_call(
        functools.partial(_flash_kernel, kind=kind, t=t, n_off=len(off)),
        grid=(B, heads),
        in_specs=[pl.BlockSpec(memory_space=pltpu.SMEM)] + in_specs,
        out_specs=own(),
        out_shape=jax.ShapeDtypeStruct((B, heads, S, LANES), BF16),
        scratch_shapes=[pltpu.VMEM((S, 2 * LANES), BF16), pltpu.VMEM((S, 2 * LANES), BF16),
                        pltpu.VMEM((S, 2 * LANES), BF16),
                        pltpu.VMEM((S, LANES), F32), pltpu.VMEM((S, 2 * LANES), F32)]
                       + [pltpu.VMEM((t, t), F32)] * FLASH_BUFS
                       + [pltpu.VMEM((t, t), BF16)] * FLASH_BUFS
                       + [pltpu.VMEM((t, LANES), F32)] * FLASH_BUFS
                       + [pltpu.VMEM((t, LANES), F32)] * FLASH_BUFS,
        compiler_params=_params("parallel", "arbitrary"),
        name=kind + "_attn",
    )(table, qa, qb, ka, kb, v)


def _outproj_kernel(x_ref, om_ref, os_ref, of_ref, w_ref, o_ref):
    a = om_ref.shape[0] * LANES
    b = a + os_ref.shape[1]
    om = jnp.concatenate([om_ref[h] for h in range(om_ref.shape[0])], axis=1)
    y = jnp.dot(om, w_ref[:a, :], preferred_element_type=F32)
    y += jnp.dot(os_ref[...], w_ref[a:b, :], preferred_element_type=F32)
    of = jnp.concatenate([of_ref[2 * c] + of_ref[2 * c + 1] for c in range(of_ref.shape[0] // 2)], axis=1)
    y += jnp.dot(of, w_ref[b:, :], preferred_element_type=F32)
    o_ref[...] = x_ref[...] + y


def _outproj(x, om, osw, of, w_out, layer, *, tm=512):
    T, D = x.shape
    S = om.shape[2]
    assert S % tm == 0
    sps = S // tm

    def row(width):
        return pl.BlockSpec((tm, width), lambda i: (i, 0))

    def heads(a):
        return pl.BlockSpec((None, a.shape[1], tm, LANES), lambda i: (i // sps, 0, i % sps, 0))

    return pl.pallas_call(
        _outproj_kernel,
        grid=(T // tm,),
        in_specs=[row(D), heads(om), row(osw.shape[1]), heads(of),
                  _resident((None,) + w_out.shape[1:], lambda i: (layer, 0, 0))],
        out_specs=row(D),
        out_shape=jax.ShapeDtypeStruct((T, D), F32),
        compiler_params=_params("parallel"),
        name="outproj",
    )(x, om, osw, of, w_out)


def kernel(x, positions, ffn1_norm, ffn1_w_gate, ffn1_w_up, ffn1_w_down, mix_norm, w_in, mla_q_norm, mla_w_q_b, mla_kv_norm, mla_w_kv_b, swa_sinks, fox_forget_bias, w_out, ffn2_norm, ffn2_w_gate, ffn2_w_up, ffn2_w_down, final_norm):
    B, S, D = x.shape
    depth = w_in.shape[0]
    T = B * S

    bf = lambda w: w.astype(BF16)
    w_in_r = _take_cols(w_in, _IN_PERM)
    wqb_r = _take_cols(mla_w_q_b, _QB_PERM)
    wkvb_r = _take_cols(mla_w_kv_b, _KVB_PERM)
    ffn_w = [(bf(ffn1_w_gate), bf(ffn1_w_up), bf(ffn1_w_down)), (bf(ffn2_w_gate), bf(ffn2_w_up), bf(ffn2_w_down))]
    ffn_g = [ffn1_norm[:, None, :], ffn2_norm[:, None, :]]
    w_out_b = bf(w_out)
    mix_g = mix_norm[:, None, :]
    qn = mla_q_norm[:, None, :]
    kvn = mla_kv_norm[:, None, :]
    fb = jnp.pad(jnp.tile(fox_forget_bias, (1, GATE_COPIES)), ((0, 0), (0, LANES - GATE_COPIES * FOX_HEADS)))[:, None, :]
    fin_g = final_norm[None, :]

    inv_freq = ROPE_THETA ** (-jnp.arange(0, 2 * HALF, 2, dtype=F32) / (2 * HALF))
    ang = positions.astype(F32).reshape(T, 1) * jnp.tile(inv_freq, LANES // HALF)
    sign = jnp.where(jnp.arange(LANES) < LANES // 2, -1.0, 1.0).astype(F32)
    cos_t = jnp.cos(ang)
    sin_t = jnp.sin(ang) * sign

    xt = x.reshape(T, D)
    sh = lambda a: a.reshape(B, S, a.shape[-1])
    for l in range(depth):
        xt = _ffn(xt, ffn_g[0], *ffn_w[0], l, fin_g, final_norm=False)
        qmn, qmp, kmn, kmr, vm, o_swa, qf, kf, vf, qaug, kaug = _mixin(
            xt, mix_g, w_in_r, qn, wqb_r, kvn, wkvb_r, cos_t, sin_t, fb, swa_sinks[l], l, batch=B)
        o_mla = _flash_attention(qmn, qmp, kmn, sh(kmr), vm, kind="mla")
        o_fox = _flash_attention(qf, sh(qaug), kf, sh(kaug), vf, kind="fox")
        xt = _outproj(xt, o_mla, o_swa, o_fox, w_out_b, l)
        xt = _ffn(xt, ffn_g[1], *ffn_w[1], l, fin_g, final_norm=(l == depth - 1))
    return xt.reshape(B, S, D)
```

```python
import functools
import math

import numpy as np
import jax
import jax.numpy as jnp
from jax import lax
from jax.experimental import pallas as pl
from jax.experimental.pallas import tpu as pltpu

RMS_EPS = 1e-6
ROPE_THETA = 10000.0

MLA_HEADS = 8
MLA_Q_LORA = 512
MLA_KV_LORA = 256
MLA_NOPE = 128
MLA_ROPE = 64
MLA_V = 128

SWA_HEADS = 8
SWA_KV_HEADS = 2
SWA_DIM = 64
WINDOW = 128

FOX_HEADS = 8
FOX_DIM = 64

LANES = 128
HALF = 32
VMEM_LIMIT = 60 * 1024 * 1024
FFN_ROW_CHUNK = 256
FLASH_LAG = 2
FLASH_BUFS = 2 * FLASH_LAG
FLASH_UNROLL = 7
NEG = -1e30
LOG2E = math.log2(math.e)
GATE_COPIES = 6

F32 = jnp.float32
BF16 = jnp.bfloat16

_O_CQ, _O_CKV, _O_KR, _O_QS, _O_KS, _O_VS, _O_QF, _O_KF, _O_VF, _O_FG, IN_COLS_R = (
    0, 512, 768, 896, 1408, 1664, 1920, 2432, 2944, 3456, 3584)


def _chunk_pair(base_a, base_b):
    r = np.arange(HALF)
    return np.concatenate([base_a + r, base_b + r, base_a + HALF + r, base_b + HALF + r])


def _in_perm():
    src = [0, 512, 768, 832, 1344, 1472, 1600, 2112, 2624, 3136]
    zero = 3144
    cols = [np.arange(src[0], src[0] + 512), np.arange(src[1], src[1] + 256)]
    cols.append(_chunk_pair(src[2], src[2]))
    for c in range(SWA_HEADS // 2):
        cols.append(_chunk_pair(src[3] + SWA_DIM * 2 * c, src[3] + SWA_DIM * (2 * c + 1)))
    for g in range(SWA_KV_HEADS):
        cols.append(_chunk_pair(src[4] + SWA_DIM * g, src[4] + SWA_DIM * g))
    for g in range(SWA_KV_HEADS):
        v = src[5] + SWA_DIM * g + np.arange(SWA_DIM)
        cols.append(np.concatenate([v, v]))
    cols += [np.arange(src[6], src[6] + 512), np.arange(src[7], src[7] + 512), np.arange(src[8], src[8] + 512)]
    gate = np.tile(np.arange(src[9], src[9] + FOX_HEADS), GATE_COPIES)
    cols.append(np.concatenate([gate, np.full(LANES - gate.shape[0], zero)]))
    out = np.concatenate(cols).astype(np.int32)
    assert out.shape[0] == IN_COLS_R
    return out


def _qb_perm():
    per = MLA_NOPE + MLA_ROPE
    cols = [per * h + np.arange(MLA_NOPE) for h in range(MLA_HEADS)]
    for c in range(MLA_HEADS // 2):
        cols.append(_chunk_pair(per * 2 * c + MLA_NOPE, per * (2 * c + 1) + MLA_NOPE))
    return np.concatenate(cols).astype(np.int32)


def _kvb_perm():
    per = MLA_NOPE + MLA_V
    k = [per * h + np.arange(MLA_NOPE) for h in range(MLA_HEADS)]
    v = [per * h + MLA_NOPE + np.arange(MLA_V) for h in range(MLA_HEADS)]
    return np.concatenate(k + v).astype(np.int32)


_IN_PERM, _QB_PERM, _KVB_PERM = _in_perm(), _qb_perm(), _kvb_perm()


def _take_cols(w, perm):
    ncol = w.shape[-1]
    same_run = lambda k: (perm[k] == ncol) if perm[k - 1] == ncol else (perm[k] == perm[k - 1] + 1 and perm[k] != ncol)
    cuts = [0] + [k for k in range(1, len(perm)) if not same_run(k)] + [len(perm)]
    parts = []
    for a, b in zip(cuts[:-1], cuts[1:]):
        if perm[a] == ncol:
            parts.append(jnp.zeros(w.shape[:-1] + (b - a,), w.dtype))
        else:
            parts.append(w[..., int(perm[a]):int(perm[a]) + (b - a)])
    return jnp.concatenate(parts, axis=-1).astype(BF16)


def _params(*sem):
    return pltpu.CompilerParams(dimension_semantics=sem, vmem_limit_bytes=VMEM_LIMIT)


def _resident(block_shape, index_map):
    return pl.BlockSpec(block_shape, index_map, pipeline_mode=pl.Buffered(1))


def _rms(x, gain):
    ms = jnp.mean(x * x, axis=-1, keepdims=True)
    return x * lax.rsqrt(ms + RMS_EPS) * gain


def _ffn_kernel(x_ref, g_ref, wg_ref, wu_ref, wd_ref, fg_ref, o_ref, h_sc, *, final_norm):
    f = pl.program_id(1)

    def row_chunks(body):
        def step(r, carry):
            body(pl.ds(pl.multiple_of(r * FFN_ROW_CHUNK, FFN_ROW_CHUNK), FFN_ROW_CHUNK))
            return carry
        lax.fori_loop(0, x_ref.shape[0] // FFN_ROW_CHUNK, step, 0)

    @pl.when(f == 0)
    def _():
        def prologue(rows):
            h_sc[rows, :] = _rms(x_ref[rows, :], g_ref[...]).astype(BF16)
            o_ref[rows, :] = jnp.zeros((FFN_ROW_CHUNK, o_ref.shape[1]), F32)
        row_chunks(prologue)

    h = h_sc[...]
    gate = jnp.dot(h, wg_ref[...], preferred_element_type=F32)
    up = jnp.dot(h, wu_ref[...], preferred_element_type=F32)
    act = (gate * (1.0 / (1.0 + jnp.exp(-gate))) * up).astype(BF16)
    o_ref[...] += jnp.dot(act, wd_ref[...], preferred_element_type=F32)

    @pl.when(f == pl.num_programs(1) - 1)
    def _():
        def epilogue(rows):
            y = x_ref[rows, :] + 0.5 * o_ref[rows, :]
            if final_norm:
                y = _rms(y, fg_ref[...])
            o_ref[rows, :] = y
        row_chunks(epilogue)


def _ffn(x, gain, wg, wu, wd, layer, final_gain, *, final_norm, tm=1024, tf=512):
    T, D = x.shape
    F = wg.shape[-1]
    assert T % tm == 0 and F % tf == 0
    return pl.pallas_call(
        functools.partial(_ffn_kernel, final_norm=final_norm),
        grid=(T // tm, F // tf),
        in_specs=[
            pl.BlockSpec((tm, D), lambda i, f: (i, 0)),
            pl.BlockSpec((None, 1, D), lambda i, f: (layer, 0, 0)),
            pl.BlockSpec((None, D, tf), lambda i, f: (layer, 0, f)),
            pl.BlockSpec((None, D, tf), lambda i, f: (layer, 0, f)),
            pl.BlockSpec((None, tf, D), lambda i, f: (layer, f, 0)),
            pl.BlockSpec((1, D), lambda i, f: (0, 0)),
        ],
        out_specs=pl.BlockSpec((tm, D), lambda i, f: (i, 0)),
        out_shape=jax.ShapeDtypeStruct((T, D), F32),
        scratch_shapes=[pltpu.VMEM((tm, D), BF16)],
        compiler_params=_params("parallel", "arbitrary"),
        name="ffn",
    )(x, gain, wg, wu, wd, final_gain)


def _rope(x, cos, sin_signed):
    return x * cos + pltpu.roll(x, 2 * HALF, axis=1) * sin_signed


def _split3(x):
    a1 = x.astype(BF16)
    r1 = x - a1.astype(F32)
    a2 = r1.astype(BF16)
    a3 = (r1 - a2.astype(F32)).astype(BF16)
    return a1, a2, a3


def _store_chunks(ref, value):
    for c in range(ref.shape[0]):
        ref[c] = value[:, LANES * c:LANES * (c + 1)]


def _swa_kv_head(hh):
    return hh // (SWA_HEADS // SWA_KV_HEADS)


def _swa_scores(q, k_heads):
    lane = lax.broadcasted_iota(jnp.int32, (1, LANES), 1)
    pair_lo = (lane % (2 * HALF)) < HALF
    zero = jnp.zeros((), BF16)
    scores = []
    for hh in range(SWA_HEADS):
        kg = k_heads[_swa_kv_head(hh)]
        ke = jnp.where(pair_lo, kg, zero) if hh % 2 == 0 else jnp.where(pair_lo, zero, kg)
        scores.append(_qk(q[:, LANES * (hh // 2):LANES * (hh // 2 + 1)], ke))
    return scores


def _swa_finish(sink_ref, scores, v_heads, has_prev):
    t = scores[0].shape[0]
    lane = lax.broadcasted_iota(jnp.int32, (1, LANES), 1)
    row = lax.broadcasted_iota(jnp.int32, (t, WINDOW + t), 0)
    col = lax.broadcasted_iota(jnp.int32, (t, WINDOW + t), 1)
    valid = (col <= row + WINDOW) & (col > row) & ((col >= WINDOW) | has_prev)
    probs, denoms = [], []
    for hh in range(SWA_HEADS):
        sink = sink_ref[hh] * LOG2E
        s = jnp.where(valid, scores[hh], NEG)
        m = jnp.maximum(jnp.max(s, axis=-1, keepdims=True), sink)
        p = jnp.exp2(s - m)
        denoms.append(jnp.sum(p, axis=-1, keepdims=True) + jnp.exp2(sink - m))
        probs.append(p.astype(BF16))
    outs = []
    for hh in range(SWA_HEADS):
        outs.append(jnp.dot(probs[hh], v_heads[_swa_kv_head(hh)], preferred_element_type=F32) / denoms[hh])
    return jnp.concatenate([jnp.where(lane < SWA_DIM, outs[2 * c], outs[2 * c + 1]) for c in range(SWA_HEADS // 2)],
                           axis=1).astype(BF16)


def _mixin_kernel(x_ref, g_ref, win_ref, qn_ref, wqb_ref, kvn_ref, wkvb_ref, cos_ref, sin_ref, fb_ref, sink_ref,
                  qmn_ref, qmp_ref, kmn_ref, kmr_ref, vm_ref, os_ref,
                  qf_ref, kf_ref, vf_ref, qaug_ref, kaug_ref, carry_sc, kprev_sc, vprev_sc, *, steps_per_seq):
    i = pl.program_id(0)
    tm = x_ref.shape[0]
    seq_start = i % steps_per_seq == 0

    @pl.when(seq_start)
    def _():
        carry_sc[...] = jnp.zeros_like(carry_sc)
        kprev_sc[...] = jnp.zeros_like(kprev_sc)
        vprev_sc[...] = jnp.zeros_like(vprev_sc)

    h = _rms(x_ref[...], g_ref[...]).astype(BF16)
    cos = cos_ref[...]
    sin = sin_ref[...]
    lane = lax.broadcasted_iota(jnp.int32, (1, LANES), 1)
    n_nope = MLA_HEADS * MLA_NOPE

    def proj(a, b):
        return jnp.dot(h, win_ref[:, a:b], preferred_element_type=F32)

    cq = _rms(proj(_O_CQ, _O_CKV), qn_ref[...]).astype(BF16)
    ckv = _rms(proj(_O_CKV, _O_KR), kvn_ref[...]).astype(BF16)
    fl = proj(_O_FG, IN_COLS_R) + fb_ref[...]
    logf = jnp.minimum(fl, 0.0) - jnp.log(1.0 + jnp.exp(-jnp.abs(fl)))

    kmr_ref[...] = _rope(proj(_O_KR, _O_QS), cos, sin).astype(BF16)
    qs = proj(_O_QS, _O_KS)
    q_swa = jnp.concatenate([(_rope(qs[:, LANES * c:LANES * (c + 1)], cos, sin) * (SWA_DIM ** -0.5 * LOG2E)).astype(BF16)
                             for c in range(SWA_HEADS // 2)], axis=1)
    ks = proj(_O_KS, _O_VS)
    vs = proj(_O_VS, _O_QF)
    k_heads, v_heads = [], []
    for g in range(SWA_KV_HEADS):
        sl = slice(LANES * g, LANES * (g + 1))
        k_own = _rope(ks[:, sl], cos, sin).astype(BF16)
        v_own = vs[:, sl].astype(BF16)
        k_heads.append(jnp.concatenate([kprev_sc[:, sl], k_own], axis=0))
        v_heads.append(jnp.concatenate([vprev_sc[:, sl], v_own], axis=0))
        kprev_sc[:, sl] = k_own[tm - WINDOW:, :]
        vprev_sc[:, sl] = v_own[tm - WINDOW:, :]
    swa_scores = _swa_scores(q_swa, k_heads)

    _store_chunks(qf_ref, (proj(_O_QF, _O_KF) * (FOX_DIM ** -0.5 * LOG2E)).astype(BF16))
    _store_chunks(kf_ref, proj(_O_KF, _O_VF).astype(BF16))
    _store_chunks(vf_ref, proj(_O_VF, _O_FG).astype(BF16))

    row = lax.broadcasted_iota(jnp.int32, (tm, tm), 0)
    col = lax.broadcasted_iota(jnp.int32, (tm, tm), 1)
    tri = (col <= row).astype(BF16)
    csum = carry_sc[...]
    for term in _split3(logf):
        csum = csum + jnp.dot(tri, term, preferred_element_type=F32)
    carry_sc[...] = csum[tm - 1:tm, :]

    c1, c2, c3 = (term.astype(F32) for term in _split3(csum * LOG2E))
    g8 = lane // FOX_HEADS
    ones = jnp.where(g8 < GATE_COPIES, 1.0, 0.0)
    qaug_ref[...] = jnp.where(g8 < 3, ones, jnp.where(g8 == 3, c1, jnp.where(g8 == 4, c2, jnp.where(g8 == 5, c3, 0.0)))).astype(BF16)
    kaug_ref[...] = jnp.where(g8 == 0, -c1, jnp.where(g8 == 1, -c2, jnp.where(g8 == 2, -c3, ones))).astype(BF16)

    q = jnp.dot(cq, wqb_ref[...], preferred_element_type=F32)
    q_scale = (MLA_NOPE + MLA_ROPE) ** -0.5 * LOG2E
    _store_chunks(qmn_ref, (q[:, :n_nope] * q_scale).astype(BF16))
    for c in range(MLA_HEADS // 2):
        qmp_ref[c] = (_rope(q[:, n_nope + LANES * c:n_nope + LANES * (c + 1)], cos, sin) * q_scale).astype(BF16)
    kv = jnp.dot(ckv, wkvb_ref[...], preferred_element_type=F32)
    _store_chunks(kmn_ref, kv[:, :n_nope].astype(BF16))
    _store_chunks(vm_ref, kv[:, n_nope:].astype(BF16))

    os_ref[...] = _swa_finish(sink_ref, swa_scores, v_heads, i % steps_per_seq > 0)


def _mixin(x, gain, w_in, qn, wqb, kvn, wkvb, cos, sin, fb, sinks, layer, *, batch, tm=256):
    T, D = x.shape
    S = T // batch
    assert S % tm == 0 and tm % WINDOW == 0
    sps = S // tm

    def row(width):
        return pl.BlockSpec((tm, width), lambda i: (i, 0))

    def wres(a):
        return _resident((None,) + a.shape[1:], lambda i: (layer, 0, 0))

    def chunked(chunks):
        return (jax.ShapeDtypeStruct((batch, chunks, S, LANES), BF16),
                pl.BlockSpec((None, chunks, tm, LANES), lambda i: (i // sps, 0, i % sps, 0)))

    def flat(width):
        return jax.ShapeDtypeStruct((T, width), BF16), row(width)

    outs = dict(qmn=chunked(MLA_HEADS), qmp=chunked(MLA_HEADS // 2), kmn=chunked(MLA_HEADS), kmr=flat(LANES),
                vm=chunked(MLA_HEADS), o_swa=flat(SWA_DIM * SWA_HEADS), qf=chunked(FOX_HEADS // 2),
                kf=chunked(FOX_HEADS // 2), vf=chunked(FOX_HEADS // 2), qaug=flat(LANES), kaug=flat(LANES))
    kv_w = LANES * SWA_KV_HEADS
    return pl.pallas_call(
        functools.partial(_mixin_kernel, steps_per_seq=sps),
        grid=(T // tm,),
        in_specs=[row(D), wres(gain), wres(w_in), wres(qn), wres(wqb), wres(kvn), wres(wkvb),
                  row(LANES), row(LANES), wres(fb), pl.BlockSpec(memory_space=pltpu.SMEM)],
        out_specs=[spec for _, spec in outs.values()],
        out_shape=[shape for shape, _ in outs.values()],
        scratch_shapes=[pltpu.VMEM((1, LANES), F32), pltpu.VMEM((WINDOW, kv_w), BF16), pltpu.VMEM((WINDOW, kv_w), BF16)],
        compiler_params=_params("arbitrary"),
        name="mixin",
    )(x, gain, w_in, qn, wqb, kvn, wkvb, cos, sin, fb, sinks)


def _qk(q, k):
    return lax.dot_general(q, k, (((1,), (1,)), ((), ())), preferred_element_type=F32)


def _causal_mask(t):
    row = lax.broadcasted_iota(jnp.int32, (t, t), 0)
    col = lax.broadcasted_iota(jnp.int32, (t, t), 1)
    return col <= row


def _flash_kernel(tab_ref, qa_ref, qb_ref, ka_ref, kb_ref, v_ref, o_ref,
                  q_sc, k_sc, v_sc, m_sc, acc_sc, *bufs, kind, t, n_off):
    h = pl.program_id(1)
    n = q_sc.shape[0] // t
    lane = lax.broadcasted_iota(jnp.int32, (1, LANES), 1)
    even = (h % 2) == 0
    zero = jnp.zeros((), BF16)

    if kind == "mla":
        mine = ((lane % (2 * HALF)) < HALF) == even
        q_sc[:, :LANES] = qa_ref[...]
        q_sc[:, LANES:] = qb_ref[...]
        k_sc[:, :LANES] = ka_ref[...]
        k_sc[:, LANES:] = jnp.where(mine, kb_ref[...], zero)
        v_sc[:, :LANES] = v_ref[...]
    else:
        mine = (lane < FOX_DIM) == even
        gate = ((lane % FOX_HEADS) == h) & (lane < GATE_COPIES * FOX_HEADS)
        q_sc[:, :LANES] = qa_ref[...]
        q_sc[:, LANES:] = jnp.where(gate, qb_ref[...], zero)
        k_sc[:, :LANES] = jnp.where(mine, ka_ref[...], zero)
        k_sc[:, LANES:] = kb_ref[...]
        v_sc[:, :LANES] = jnp.where(mine, v_ref[...], zero)
    v_sc[:, LANES:] = jnp.ones((v_sc.shape[0], LANES), BF16)

    def rows(j):
        return pl.ds(pl.multiple_of(j * t, t), t)

    s_bufs, p_bufs, a_bufs, r_bufs = (bufs[FLASH_BUFS * g:FLASH_BUFS * (g + 1)] for g in range(4))

    def scores(i, j, slot):
        s = _qk(q_sc[rows(i), :], k_sc[rows(j), :])
        s_bufs[slot][...] = s
        r_bufs[slot][...] = functools.reduce(jnp.maximum, [s[:, LANES * c:LANES * (c + 1)] for c in range(t // LANES)])

    def softmax(par, i, first):
        s = s_bufs[par][...]
        if first:
            s = jnp.where(_causal_mask(t), s, NEG)
        tiles = [s[:, LANES * c:LANES * (c + 1)] for c in range(t // LANES)]
        tile_max = functools.reduce(jnp.maximum, tiles) if first else r_bufs[par][...]
        m_new = jnp.broadcast_to(jnp.max(tile_max, axis=-1, keepdims=True), (t, LANES))
        if not first:
            m_old = m_sc[rows(i), :]
            m_new = jnp.maximum(m_old, m_new)
            a_bufs[par][...] = jnp.exp2(m_old - m_new)
        for c, tile in enumerate(tiles):
            p_bufs[par][:, LANES * c:LANES * (c + 1)] = jnp.exp2(tile - m_new).astype(BF16)
        m_sc[rows(i), :] = m_new

    def accumulate(par, i, j, first):
        pv = jnp.dot(p_bufs[par][...], v_sc[rows(j), :], preferred_element_type=F32)
        if first:
            acc_sc[rows(i), :] = pv
        else:
            a = a_bufs[par][...]
            acc_sc[rows(i), :] = jnp.concatenate([a, a], axis=1) * acc_sc[rows(i), :] + pv

    def pipeline(base, count, first, lag, unroll=1):
        nb = 2 * lag
        blk = lambda a: (tab_ref[0, base + a], tab_ref[1, base + a])

        def step(a, u, with_acc=True):
            if with_acc:
                accumulate((u - lag) % nb, *blk(a - lag), first)
            scores(*blk(a + lag), (u + lag) % nb)
            softmax(u, blk(a)[0], first)

        for a in range(lag):
            scores(*blk(a), a)
        for a in range(min(lag, count)):
            step(a, a, with_acc=False)
        main = max(count - lag, 0)

        span = nb * unroll

        def body(k, carry):
            for u in range(span):
                step(lag + span * k + u, (lag + u) % nb)
            return carry

        lax.fori_loop(0, main // span, body, 0)
        for a in range(lag + main // span * span, count):
            step(a, a % nb)
        for a in range(main, count):
            accumulate(a % nb, *blk(a), first)

    pipeline(0, n, True, 1)
    pipeline(n, n_off, False, FLASH_LAG, unroll=FLASH_UNROLL)

    def finish(i, carry):
        o_ref[rows(i), :] = (acc_sc[rows(i), :LANES] / acc_sc[rows(i), LANES:]).astype(o_ref.dtype)
        return carry

    lax.fori_loop(0, n, finish, 0)


def _flash_attention(qa, qb, ka, kb, v, *, kind, t=512):
    B, _, S, _ = qa.shape
    heads = MLA_HEADS if kind == "mla" else FOX_HEADS
    n = S // t
    assert S % t == 0 and n % 2 == 0
    diag = [(i, i) for i in range(n)]
    off = [(i, j) for j in range(n - 1) for i in range(j + 1, n)]
    table = jnp.asarray(np.array(diag + off + off[-1:] * (2 * FLASH_LAG), np.int32).T)
    shared = lambda **kw: pl.BlockSpec((None, S, LANES), lambda b, h: (b, 0, 0), **kw)
    own = lambda **kw: pl.BlockSpec((None, None, S, LANES), lambda b, h: (b, h, 0, 0), **kw)
    pair = lambda **kw: pl.BlockSpec((None, None, S, LANES), lambda b, h: (b, h // 2, 0, 0), **kw)
    single = dict(pipeline_mode=pl.Buffered(1))
    if kind == "mla":
        in_specs = [own(**single), pair(**single), own(), shared(), own()]
    else:
        in_specs = [pair(**single), shared(**single), pair(), shared(), pair()]
    return pl.pallas_call(
        functools.partial(_flash_kernel, kind=kind, t=t, n_off=len(off)),
        grid=(B, heads),
        in_specs=[pl.BlockSpec(memory_space=pltpu.SMEM)] + in_specs,
        out_specs=own(),
        out_shape=jax.ShapeDtypeStruct((B, heads, S, LANES), BF16),
        scratch_shapes=[pltpu.VMEM((S, 2 * LANES), BF16), pltpu.VMEM((S, 2 * LANES), BF16),
                        pltpu.VMEM((S, 2 * LANES), BF16),
                        pltpu.VMEM((S, LANES), F32), pltpu.VMEM((S, 2 * LANES), F32)]
                       + [pltpu.VMEM((t, t), F32)] * FLASH_BUFS
                       + [pltpu.VMEM((t, t), BF16)] * FLASH_BUFS
                       + [pltpu.VMEM((t, LANES), F32)] * FLASH_BUFS
                       + [pltpu.VMEM((t, LANES), F32)] * FLASH_BUFS,
        compiler_params=_params("parallel", "arbitrary"),
        name=kind + "_attn",
    )(table, qa, qb, ka, kb, v)


def _outproj_kernel(x_ref, om_ref, os_ref, of_ref, w_ref, o_ref):
    a = om_ref.shape[0] * LANES
    b = a + os_ref.shape[1]
    om = jnp.concatenate([om_ref[h] for h in range(om_ref.shape[0])], axis=1)
    y = jnp.dot(om, w_ref[:a, :], preferred_element_type=F32)
    y += jnp.dot(os_ref[...], w_ref[a:b, :], preferred_element_type=F32)
    of = jnp.concatenate([of_ref[2 * c] + of_ref[2 * c + 1] for c in range(of_ref.shape[0] // 2)], axis=1)
    y += jnp.dot(of, w_ref[b:, :], preferred_element_type=F32)
    o_ref[...] = x_ref[...] + y


def _outproj(x, om, osw, of, w_out, layer, *, tm=512):
    T, D = x.shape
    S = om.shape[2]
    assert S % tm == 0
    sps = S // tm

    def row(width):
        return pl.BlockSpec((tm, width), lambda i: (i, 0))

    def heads(a):
        return pl.BlockSpec((None, a.shape[1], tm, LANES), lambda i: (i // sps, 0, i % sps, 0))

    return pl.pallas_call(
        _outproj_kernel,
        grid=(T // tm,),
        in_specs=[row(D), heads(om), row(osw.shape[1]), heads(of),
                  _resident((None,) + w_out.shape[1:], lambda i: (layer, 0, 0))],
        out_specs=row(D),
        out_shape=jax.ShapeDtypeStruct((T, D), F32),
        compiler_params=_params("parallel"),
        name="outproj",
    )(x, om, osw, of, w_out)


def kernel(x, positions, ffn1_norm, ffn1_w_gate, ffn1_w_up, ffn1_w_down, mix_norm, w_in, mla_q_norm, mla_w_q_b, mla_kv_norm, mla_w_kv_b, swa_sinks, fox_forget_bias, w_out, ffn2_norm, ffn2_w_gate, ffn2_w_up, ffn2_w_down, final_norm):
    B, S, D = x.shape
    depth = w_in.shape[0]
    T = B * S

    bf = lambda w: w.astype(BF16)
    w_in_r = _take_cols(w_in, _IN_PERM)
    wqb_r = _take_cols(mla_w_q_b, _QB_PERM)
    wkvb_r = _take_cols(mla_w_kv_b, _KVB_PERM)
    ffn_w = [(bf(ffn1_w_gate), bf(ffn1_w_up), bf(ffn1_w_down)), (bf(ffn2_w_gate), bf(ffn2_w_up), bf(ffn2_w_down))]
    ffn_g = [ffn1_norm[:, None, :], ffn2_norm[:, None, :]]
    w_out_b = bf(w_out)
    mix_g = mix_norm[:, None, :]
    qn = mla_q_norm[:, None, :]
    kvn = mla_kv_norm[:, None, :]
    fb = jnp.pad(jnp.tile(fox_forget_bias, (1, GATE_COPIES)), ((0, 0), (0, LANES - GATE_COPIES * FOX_HEADS)))[:, None, :]
    fin_g = final_norm[None, :]

    inv_freq = ROPE_THETA ** (-jnp.arange(0, 2 * HALF, 2, dtype=F32) / (2 * HALF))
    ang = positions.astype(F32).reshape(T, 1) * jnp.tile(inv_freq, LANES // HALF)
    sign = jnp.where(jnp.arange(LANES) < LANES // 2, -1.0, 1.0).astype(F32)
    cos_t = jnp.cos(ang)
    sin_t = jnp.sin(ang) * sign

    xt = x.reshape(T, D)
    sh = lambda a: a.reshape(B, S, a.shape[-1])
    for l in range(depth):
        xt = _ffn(xt, ffn_g[0], *ffn_w[0], l, fin_g, final_norm=False)
        qmn, qmp, kmn, kmr, vm, o_swa, qf, kf, vf, qaug, kaug = _mixin(
            xt, mix_g, w_in_r, qn, wqb_r, kvn, wkvb_r, cos_t, sin_t, fb, swa_sinks[l], l, batch=B)
        o_mla = _flash_attention(qmn, qmp, kmn, sh(kmr), vm, kind="mla")
        o_fox = _flash_attention(qf, sh(qaug), kf, sh(kaug), vf, kind="fox")
        xt = _outproj(xt, o_mla, o_swa, o_fox, w_out_b, l)
        xt = _ffn(xt, ffn_g[1], *ffn_w[1], l, fin_g, final_norm=(l == depth - 1))
    return xt.reshape(B, S, D)
```
